```python
import math
import jax
import jax.numpy as jnp
from jax import lax
import numpy as np


D_MODEL = 2048
BATCH = 4
SEQ = 4096
DEPTH = 2
DEC_BATCH = 16
DEC_SEQ = 2048
PAST_LEN = 128

GRID_W = 64
Q_BLOCK = 128
HEAD_DIM = 128
SSM_WIDTH = D_MODEL // 2
SSM_GROUP = 16
SSM_GROUPS = SSM_WIDTH // SSM_GROUP
SSM_STATE = 64
ATT_WIDTH = D_MODEL - SSM_WIDTH
B_HEADS = ATT_WIDTH // HEAD_DIM
B_KV_HEADS = max(1, B_HEADS // 4)
B_GROUP = B_HEADS // B_KV_HEADS
AXIAL_THETA = 10000.0
KV_WIDTH = B_KV_HEADS * HEAD_DIM
EVEN_SPLITS = (SSM_WIDTH, 2 * SSM_WIDTH, 2 * SSM_WIDTH + ATT_WIDTH, 2 * SSM_WIDTH + ATT_WIDTH + KV_WIDTH, 2 * SSM_WIDTH + ATT_WIDTH + 2 * KV_WIDTH)
EVEN_IN = 2 * SSM_WIDTH + 2 * ATT_WIDTH + 2 * KV_WIDTH
EVEN_MIX = SSM_WIDTH + ATT_WIDTH
C_HEADS = D_MODEL // HEAD_DIM
C_WIDTH = C_HEADS * HEAD_DIM
C_WINDOWS = (128, 512, 2048)
C_DILATIONS = (1, 4, 16)
C_PAD = max(C_WINDOWS) // 2
ODD_IN = 4 * C_WIDTH
ROPE_THETA = 500000.0
ROPE_DIMS = HEAD_DIM // 4
N_EVEN = (DEPTH + 1) // 2
N_ODD = DEPTH // 2
DN_ALPHA = (2 * DEPTH) ** 0.25
DN_BETA = (8 * DEPTH) ** -0.25
LN_EPS = 1e-5
QK_EPS = 1e-6

kernel_name = "hybrid_s5_axialgqa_dilated_encoder"


def _layernorm(x, g, b):
    xf = x.astype(jnp.float32)
    mu = jnp.mean(xf, axis=-1, keepdims=True)
    var = jnp.mean(jnp.square(xf - mu), axis=-1, keepdims=True)
    return ((xf - mu) * lax.rsqrt(var + LN_EPS) * g.astype(jnp.float32) + b.astype(jnp.float32)).astype(x.dtype)


def _rmsnorm(x, g):
    xf = x.astype(jnp.float32)
    ms = jnp.mean(jnp.square(xf), axis=-1, keepdims=True)
    return (xf * lax.rsqrt(ms + QK_EPS) * g.astype(jnp.float32)).astype(x.dtype)


def _rotary(x, pos, theta):
    n = x.shape[-1]
    inv = theta ** (-jnp.arange(0, n, 2, dtype=jnp.float32) / n)
    ang = pos[:, None] * inv[None, :]
    shp = (ang.shape[0],) + (1,) * (x.ndim - 3) + (n // 2,)
    cos = jnp.cos(ang).reshape(shp)
    sin = jnp.sin(ang).reshape(shp)
    xf = x.astype(jnp.float32)
    x1, x2 = xf[..., : n // 2], xf[..., n // 2 :]
    return jnp.concatenate([x1 * cos - x2 * sin, x2 * cos + x1 * sin], axis=-1).astype(x.dtype)


def _complex_combine(e1, e2):
    a1r, a1i, b1r, b1i = e1
    a2r, a2i, b2r, b2i = e2
    ar = a1r * a2r - a1i * a2i
    ai = a1r * a2i + a1i * a2r
    br = a2r * b1r - a2i * b1i + b2r
    bi = a2r * b1i + a2i * b1r + b2i
    return (ar, ai, br, bi)


def _s5_mixer(u, a_re, a_im, log_dt, b_re, b_im, c_re, c_im, d_skip, glu_w, glu_b):
    bsz, t_len, _ = u.shape
    uf = u.astype(jnp.float32).reshape(bsz, t_len, SSM_GROUPS, SSM_GROUP)
    ar_c = a_re.astype(jnp.float32)
    ai_c = a_im.astype(jnp.float32)
    dt = jnp.exp(log_dt.astype(jnp.float32))[..., None]
    mag = jnp.exp(ar_c * dt)
    lb_re = mag * jnp.cos(ai_c * dt)
    lb_im = mag * jnp.sin(ai_c * dt)
    nr = lb_re - 1.0
    den = jnp.square(ar_c) + jnp.square(ai_c)
    f_re = (nr * ar_c + lb_im * ai_c) / den
    f_im = (lb_im * ar_c - nr * ai_c) / den
    br = b_re.astype(jnp.float32)
    bi = b_im.astype(jnp.float32)
    bb_re = f_re[..., None] * br - f_im[..., None] * bi
    bb_im = f_re[..., None] * bi + f_im[..., None] * br
    y = d_skip.astype(jnp.float32).reshape(SSM_GROUPS, SSM_GROUP) * uf
    for direction, rev in ((0, False), (1, True)):
        xr_in = jnp.einsum('btgc,gpc->btgp', uf, bb_re[direction])
        xi_in = jnp.einsum('btgc,gpc->btgp', uf, bb_im[direction])
        ar = jnp.broadcast_to(lb_re[direction][None, None], (1, t_len, SSM_GROUPS, SSM_STATE))
        ai = jnp.broadcast_to(lb_im[direction][None, None], (1, t_len, SSM_GROUPS, SSM_STATE))
        _, _, xr, xi = lax.associative_scan(_complex_combine, (ar, ai, xr_in, xi_in), reverse=rev, axis=1)
        y = y + jnp.einsum('btgp,gcp->btgc', xr, c_re[direction].astype(jnp.float32)) - jnp.einsum('btgp,gcp->btgc', xi, c_im[direction].astype(jnp.float32))
    y = jax.nn.gelu(y.reshape(bsz, t_len, SSM_WIDTH))
    y = y * jax.nn.sigmoid(y @ glu_w.astype(jnp.float32) + glu_b.astype(jnp.float32))
    return y.astype(u.dtype)


def _axial_gqa(q, k, v, qn_g, kn_g, row, col):
    bsz, t_len, _ = q.shape
    q = _rmsnorm(q.reshape(bsz, t_len, B_KV_HEADS, B_GROUP, HEAD_DIM), qn_g)
    k = _rmsnorm(k.reshape(bsz, t_len, B_KV_HEADS, HEAD_DIM), kn_g)
    v = v.reshape(bsz, t_len, B_KV_HEADS, HEAD_DIM)
    half = HEAD_DIM // 2
    q = jnp.concatenate([_rotary(q[..., :half], row, AXIAL_THETA), _rotary(q[..., half:], col, AXIAL_THETA)], axis=-1)
    k = jnp.concatenate([_rotary(k[..., :half], row, AXIAL_THETA), _rotary(k[..., half:], col, AXIAL_THETA)], axis=-1)
    nb = t_len // Q_BLOCK
    qb = q.reshape(bsz, nb, Q_BLOCK, B_KV_HEADS, B_GROUP, HEAD_DIM).transpose(1, 0, 2, 3, 4, 5)
    kf = k.astype(jnp.float32)
    vf = v.astype(jnp.float32)
    scale = HEAD_DIM ** -0.5

    def block(qblk):
        s = jnp.einsum('bqhge,bkhe->bhgqk', qblk.astype(jnp.float32), kf) * scale
        p = jax.nn.softmax(s, axis=-1)
        return jnp.einsum('bhgqk,bkhe->bqhge', p, vf)

    o = lax.map(block, qb)
    return o.transpose(1, 0, 2, 3, 4, 5).reshape(bsz, t_len, ATT_WIDTH).astype(v.dtype)


def _dilated_mixture(q, k, v, pos):
    bsz, t_len, _ = q.shape
    q = q.reshape(bsz, t_len, C_HEADS, HEAD_DIM)
    k = k.reshape(bsz, t_len, C_HEADS, HEAD_DIM)
    v = v.reshape(bsz, t_len, C_HEADS, HEAD_DIM)
    q = jnp.concatenate([_rotary(q[..., :ROPE_DIMS], pos, ROPE_THETA), q[..., ROPE_DIMS:]], axis=-1)
    k = jnp.concatenate([_rotary(k[..., :ROPE_DIMS], pos, ROPE_THETA), k[..., ROPE_DIMS:]], axis=-1)
    kp = jnp.pad(k.astype(jnp.float32), ((0, 0), (C_PAD, C_PAD), (0, 0), (0, 0)))
    vp = jnp.pad(v.astype(jnp.float32), ((0, 0), (C_PAD, C_PAD), (0, 0), (0, 0)))
    nb = t_len // Q_BLOCK
    qb = q.astype(jnp.float32).reshape(bsz, nb, Q_BLOCK, C_HEADS, HEAD_DIM).transpose(1, 0, 2, 3, 4)
    q0s = jnp.arange(nb, dtype=jnp.int32) * Q_BLOCK
    scale = HEAD_DIM ** -0.5

    def block(args):
        qblk, q0 = args
        outs = []
        lses = []
        for w, d in zip(C_WINDOWS, C_DILATIONS):
            hw = w // 2
            seg = Q_BLOCK + 2 * hw
            na = Q_BLOCK // d
            nc = seg // d
            n_span = 2 * hw // d
            start = q0 + (C_PAD - hw)
            ks = lax.dynamic_slice_in_dim(kp, start, seg, axis=1).reshape(bsz, nc, d, C_HEADS, HEAD_DIM)
            vs = lax.dynamic_slice_in_dim(vp, start, seg, axis=1).reshape(bsz, nc, d, C_HEADS, HEAD_DIM)
            qs = qblk.reshape(bsz, na, d, C_HEADS, HEAD_DIM)
            s = jnp.einsum('barne,bcrne->brnac', qs, ks) * scale
            a_i = jnp.arange(na)[:, None]
            c_i = jnp.arange(nc)[None, :]
            rel = c_i - a_i
            in_win = (rel >= 0) & (rel <= n_span)
            kpos = q0 - hw + c_i[None] * d + jnp.arange(d)[:, None, None]
            valid = (kpos >= 0) & (kpos < t_len)
            mask = (in_win[None] & valid)[None, :, None]
            s = jnp.where(mask, s, -jnp.inf)
            m = jnp.max(s, axis=-1, keepdims=True)
            p = jnp.exp(s - m)
            l = jnp.sum(p, axis=-1)
            o = jnp.einsum('brnac,bcrne->barne', p, vs) / l.transpose(0, 3, 1, 2)[..., None]
            lse = (m[..., 0] + jnp.log(l)).transpose(0, 3, 1, 2)
            outs.append(o.reshape(bsz, Q_BLOCK, C_HEADS, HEAD_DIM))
            lses.append(lse.reshape(bsz, Q_BLOCK, C_HEADS))
        wts = jax.nn.softmax(jnp.stack(lses, axis=0), axis=0)
        return jnp.sum(wts[..., None] * jnp.stack(outs, axis=0), axis=0)

    o = lax.map(block, (qb, q0s))
    return o.transpose(1, 0, 2, 3, 4).reshape(bsz, t_len, C_WIDTH).astype(v.dtype)


def setup_inputs(seed: int = 0) -> dict:
    key = jax.random.key(seed)
    ks = jax.random.split(key, 24)
    f32 = jnp.float32
    x_prompt = jax.random.normal(ks[0], (BATCH, SEQ, D_MODEL), f32)
    x_sample = jax.random.normal(ks[1], (DEC_BATCH, DEC_SEQ, D_MODEL), f32)
    even_w_in = jax.random.normal(ks[2], (N_EVEN, D_MODEL, EVEN_IN), f32) * D_MODEL ** -0.5
    even_w_out = jax.random.normal(ks[3], (N_EVEN, EVEN_MIX, D_MODEL), f32) * (EVEN_MIX ** -0.5 * DN_BETA)
    ssm_a_re = -0.5 + 0.01 * jax.random.normal(ks[4], (N_EVEN, 2, SSM_GROUPS, SSM_STATE), f32)
    ssm_a_im = jnp.pi * jnp.arange(SSM_STATE, dtype=f32) + 0.01 * jax.random.normal(ks[5], (N_EVEN, 2, SSM_GROUPS, SSM_STATE), f32)
    ssm_log_dt = jax.random.uniform(ks[6], (N_EVEN, 2, SSM_GROUPS), f32, minval=math.log(1e-3), maxval=math.log(1e-1))
    ssm_b_re = jax.random.normal(ks[7], (N_EVEN, SSM_GROUPS, SSM_STATE, SSM_GROUP), f32) * (2 * SSM_GROUP) ** -0.5
    ssm_b_im = jax.random.normal(ks[8], (N_EVEN, SSM_GROUPS, SSM_STATE, SSM_GROUP), f32) * (2 * SSM_GROUP) ** -0.5
    ssm_c_re = jax.random.normal(ks[9], (N_EVEN, 2, SSM_GROUPS, SSM_GROUP, SSM_STATE), f32) * SSM_STATE ** -0.5
    ssm_c_im = jax.random.normal(ks[10], (N_EVEN, 2, SSM_GROUPS, SSM_GROUP, SSM_STATE), f32) * SSM_STATE ** -0.5
    ssm_d = jax.random.normal(ks[11], (N_EVEN, SSM_WIDTH), f32)
    ssm_glu_w = jax.random.normal(ks[12], (N_EVEN, SSM_WIDTH, SSM_WIDTH), f32) * SSM_WIDTH ** -0.5
    ssm_glu_b = 0.01 * jax.random.normal(ks[13], (N_EVEN, SSM_WIDTH), f32)
    attn_q_norm = 1.0 + 0.01 * jax.random.normal(ks[14], (N_EVEN, HEAD_DIM), f32)
    attn_k_norm = 1.0 + 0.01 * jax.random.normal(ks[15], (N_EVEN, HEAD_DIM), f32)
    odd_w_in = jax.random.normal(ks[16], (N_ODD, D_MODEL, ODD_IN), f32) * D_MODEL ** -0.5
    odd_w_out = jax.random.normal(ks[17], (N_ODD, C_WIDTH, D_MODEL), f32) * (C_WIDTH ** -0.5 * DN_BETA)
    ln_g = 1.0 + 0.01 * jax.random.normal(ks[18], (DEPTH, D_MODEL), f32)
    ln_b = 0.01 * jax.random.normal(ks[19], (DEPTH, D_MODEL), f32)
    return {"x_prompt": x_prompt, "x_sample": x_sample, "even_w_in": even_w_in, "even_w_out": even_w_out, "ssm_a_re": ssm_a_re, "ssm_a_im": ssm_a_im, "ssm_log_dt": ssm_log_dt, "ssm_b_re": ssm_b_re, "ssm_b_im": ssm_b_im, "ssm_c_re": ssm_c_re, "ssm_c_im": ssm_c_im, "ssm_d": ssm_d, "ssm_glu_w": ssm_glu_w, "ssm_glu_b": ssm_glu_b, "attn_q_norm": attn_q_norm, "attn_k_norm": attn_k_norm, "odd_w_in": odd_w_in, "odd_w_out": odd_w_out, "ln_g": ln_g, "ln_b": ln_b}


def reference(x_prompt, x_sample, even_w_in, even_w_out, ssm_a_re, ssm_a_im, ssm_log_dt, ssm_b_re, ssm_b_im, ssm_c_re, ssm_c_im, ssm_d, ssm_glu_w, ssm_glu_b, attn_q_norm, attn_k_norm, odd_w_in, odd_w_out, ln_g, ln_b):
    def trunk(x):
        t_len = x.shape[1]
        rows = t_len // GRID_W
        grid_r, grid_c = jnp.meshgrid(jnp.arange(rows, dtype=jnp.float32), jnp.arange(GRID_W, dtype=jnp.float32), indexing='ij')
        row = grid_r.reshape(-1)
        col = grid_c.reshape(-1)
        pos = jnp.arange(t_len, dtype=jnp.float32)
        for layer in range(DEPTH):
            i = layer // 2
            if layer % 2 == 0:
                h = x @ even_w_in[i]
                a_u, a_g, q, k, v, b_g = jnp.split(h, EVEN_SPLITS, axis=-1)
                ya = _s5_mixer(a_u, ssm_a_re[i], ssm_a_im[i], ssm_log_dt[i], ssm_b_re[i], ssm_b_im[i], ssm_c_re[i], ssm_c_im[i], ssm_d[i], ssm_glu_w[i], ssm_glu_b[i])
                yb = _axial_gqa(q, k, v, attn_q_norm[i], attn_k_norm[i], row, col)
                y = jnp.concatenate([ya * jax.nn.silu(a_g), yb * jax.nn.silu(b_g)], axis=-1) @ even_w_out[i]
            else:
                h = x @ odd_w_in[i]
                q, k, v, g = jnp.split(h, 4, axis=-1)
                y = (_dilated_mixture(q, k, v, pos) * jax.nn.silu(g)) @ odd_w_out[i]
            x = _layernorm(DN_ALPHA * x + y, ln_g[layer], ln_b[layer])
        return x

    y_prompt = trunk(x_prompt)
    y_sample = trunk(x_sample)
    return (y_prompt, y_sample)
```

```python
import functools
import math

import jax
import jax.numpy as jnp
from jax import lax
from jax.experimental import pallas as pl
from jax.experimental.pallas import tpu as pltpu

F32 = jnp.float32
BF16 = jnp.bfloat16

D_MODEL = 2048
HEAD_DIM = 128
GRID_W = 64
SSM_WIDTH = 1024
SSM_GROUP = 16
SSM_GROUPS = 64
SSM_STATE = 64
ATT_WIDTH = 1024
B_KV_HEADS = 2
B_GROUP = 4
KV_WIDTH = B_KV_HEADS * HEAD_DIM
AXIAL_THETA = 10000.0
C_HEADS = 16
C_DILATIONS = (1, 4, 16)
C_HALF_SPAN = 64
ROPE_THETA = 500000.0
ROPE_DIMS = 32
DEPTH = 2
DN_ALPHA = (2 * DEPTH) ** 0.25
LN_EPS = 1e-5
QK_EPS = 1e-6
ATT_SCALE = HEAD_DIM ** -0.5

LANES = 128
SUBLANES = 8
VMEM_LIMIT_BYTES = 56 * 1024 * 1024

S5_CHUNK = 16
S5_OCTETS = SSM_GROUPS // 8
S5_FEAT = S5_CHUNK * LANES
S5_HALF = 8 * SSM_STATE
S5_ROWS = 512


def _cparams(sem):
    return pltpu.CompilerParams(dimension_semantics=sem, vmem_limit_bytes=VMEM_LIMIT_BYTES)


def _axial_tables(t_len):
    pos = jnp.arange(t_len, dtype=jnp.int32)
    row = (pos // GRID_W).astype(F32)
    col = (pos % GRID_W).astype(F32)
    n = HEAD_DIM // 2
    inv = AXIAL_THETA ** (-jnp.arange(0, n, 2, dtype=F32) / n)
    ar = row[:, None] * inv[None, :]
    ac = col[:, None] * inv[None, :]
    cos = jnp.concatenate([jnp.cos(ar), jnp.cos(ar), jnp.cos(ac), jnp.cos(ac)], axis=-1)
    sin = jnp.concatenate([-jnp.sin(ar), jnp.sin(ar), -jnp.sin(ac), jnp.sin(ac)], axis=-1)
    return cos, sin


def _partial_tables(t_len):
    pos = jnp.arange(t_len, dtype=F32)
    n = ROPE_DIMS
    inv = ROPE_THETA ** (-jnp.arange(0, n, 2, dtype=F32) / n)
    ang = pos[:, None] * inv[None, :]
    rest = HEAD_DIM - n
    cos = jnp.concatenate([jnp.cos(ang), jnp.cos(ang), jnp.ones((t_len, rest), F32)], axis=-1)
    sin = jnp.concatenate([-jnp.sin(ang), jnp.sin(ang), jnp.zeros((t_len, rest), F32)], axis=-1)
    return cos, sin


def _rotate_pairs(x, cos, sin, half, period):
    lane = lax.broadcasted_iota(jnp.int32, x.shape, x.ndim - 1)
    up = pltpu.roll(x, LANES - half, x.ndim - 1)
    dn = pltpu.roll(x, half, x.ndim - 1)
    partner = jnp.where((lane % period) < half, up, dn)
    return x * cos + partner * sin


PROJ_TM = 1024


def _proj_au_kernel(x_ref, w_ref, o_ref):
    res = jnp.dot(x_ref[...].astype(BF16), w_ref[...], preferred_element_type=F32)
    for g in range(S5_OCTETS):
        o_ref[g] = res[:, g * LANES:(g + 1) * LANES].astype(o_ref.dtype)


def _proj_au(x2d, w_au):
    n_tok = x2d.shape[0]
    tm = 512
    return pl.pallas_call(
        _proj_au_kernel,
        grid=(n_tok // tm,),
        in_specs=[pl.BlockSpec((tm, D_MODEL), lambda i: (i, 0)),
                  pl.BlockSpec((D_MODEL, SSM_WIDTH), lambda i: (0, 0))],
        out_specs=pl.BlockSpec((S5_OCTETS, tm, LANES), lambda i: (0, i, 0)),
        out_shape=jax.ShapeDtypeStruct((S5_OCTETS, n_tok, LANES), BF16),
        compiler_params=_cparams(("parallel",)),
        name="proj_au",
    )(x2d, w_au)


P0_TN = 512
P0_Q_BLOCKS = (2, 3)
P0_KV_BLOCK = 4


def _proj0_kernel(x_ref, w_ref, cos_ref, sin_ref, qg_ref, kg_ref, o_ref, xb_ref):
    j = pl.program_id(1)

    @pl.when(j == 0)
    def _():
        xb_ref[...] = x_ref[...].astype(BF16)

    res = jnp.dot(xb_ref[...], w_ref[...], preferred_element_type=F32)

    def norm_rot(x, gain, scale):
        ms = jnp.mean(x * x, axis=-1, keepdims=True)
        xn = x * lax.rsqrt(ms + QK_EPS) * gain
        return _rotate_pairs(xn, cos_ref[...], sin_ref[...], HEAD_DIM // 4, HEAD_DIM // 2) * scale

    is_q = (j == P0_Q_BLOCKS[0]) | (j == P0_Q_BLOCKS[1])
    is_kv = j == P0_KV_BLOCK

    @pl.when(is_q)
    def _():
        for h in range(P0_TN // HEAD_DIM):
            sl = slice(h * HEAD_DIM, (h + 1) * HEAD_DIM)
            o_ref[:, sl] = norm_rot(res[:, sl], qg_ref[...], ATT_SCALE).astype(o_ref.dtype)

    @pl.when(is_kv)
    def _():
        for h in range(B_KV_HEADS):
            sl = slice(h * HEAD_DIM, (h + 1) * HEAD_DIM)
            o_ref[:, sl] = norm_rot(res[:, sl], kg_ref[...], 1.0).astype(o_ref.dtype)
        o_ref[:, KV_WIDTH:] = res[:, KV_WIDTH:].astype(o_ref.dtype)

    @pl.when(jnp.logical_not(is_q | is_kv))
    def _():
        o_ref[...] = res.astype(o_ref.dtype)


def _proj0(x2d, w_rest, cos, sin, qn_g, kn_g, t_len):
    n_tok = x2d.shape[0]
    n_cols = w_rest.shape[1]
    tm = min(PROJ_TM, t_len)
    tpb = t_len // tm
    return pl.pallas_call(
        _proj0_kernel,
        grid=(n_tok // tm, n_cols // P0_TN),
        in_specs=[pl.BlockSpec((tm, D_MODEL), lambda i, j: (i, 0)),
                  pl.BlockSpec((D_MODEL, P0_TN), lambda i, j: (0, j)),
                  pl.BlockSpec((tm, HEAD_DIM), lambda i, j: (i % tpb, 0)),
                  pl.BlockSpec((tm, HEAD_DIM), lambda i, j: (i % tpb, 0)),
                  pl.BlockSpec((1, HEAD_DIM), lambda i, j: (0, 0)),
                  pl.BlockSpec((1, HEAD_DIM), lambda i, j: (0, 0))],
        out_specs=pl.BlockSpec((tm, P0_TN), lambda i, j: (i, j)),
        out_shape=jax.ShapeDtypeStruct((n_tok, n_cols), BF16),
        scratch_shapes=[pltpu.VMEM((tm, D_MODEL), BF16)],
        compiler_params=_cparams(("parallel", "arbitrary")),
        name="proj0",
    )(x2d, w_rest, cos, sin, qn_g, kn_g)


P1_TN = 1024
P1_ROT_BLOCKS = 4
P1_Q_BLOCKS = 2


def _proj1_kernel(x_ref, w_ref, cos_ref, sin_ref, o_ref, xb_ref):
    j = pl.program_id(1)

    @pl.when(j == 0)
    def _():
        xb_ref[...] = x_ref[...].astype(BF16)

    res = jnp.dot(xb_ref[...], w_ref[...], preferred_element_type=F32)

    @pl.when(j < P1_ROT_BLOCKS)
    def _():
        scale = jnp.where(j < P1_Q_BLOCKS, ATT_SCALE, 1.0).astype(F32)
        for h in range(P1_TN // HEAD_DIM):
            sl = slice(h * HEAD_DIM, (h + 1) * HEAD_DIM)
            rot = _rotate_pairs(res[:, sl], cos_ref[...], sin_ref[...], ROPE_DIMS // 2, HEAD_DIM)
            o_ref[:, sl] = (rot * scale).astype(o_ref.dtype)

    @pl.when(j >= P1_ROT_BLOCKS)
    def _():
        o_ref[...] = res.astype(o_ref.dtype)


def _proj1(x2d, w, cos, sin, t_len):
    n_tok = x2d.shape[0]
    n_cols = w.shape[1]
    tm = min(PROJ_TM, t_len)
    tpb = t_len // tm
    return pl.pallas_call(
        _proj1_kernel,
        grid=(n_tok // tm, n_cols // P1_TN),
        in_specs=[pl.BlockSpec((tm, D_MODEL), lambda i, j: (i, 0)),
                  pl.BlockSpec((D_MODEL, P1_TN), lambda i, j: (0, j)),
                  pl.BlockSpec((tm, HEAD_DIM), lambda i, j: (i % tpb, 0)),
                  pl.BlockSpec((tm, HEAD_DIM), lambda i, j: (i % tpb, 0))],
        out_specs=pl.BlockSpec((tm, P1_TN), lambda i, j: (i, j)),
        out_shape=jax.ShapeDtypeStruct((n_tok, n_cols), BF16),
        scratch_shapes=[pltpu.VMEM((tm, D_MODEL), BF16)],
        compiler_params=_cparams(("parallel", "arbitrary")),
        name="proj1",
    )(x2d, w, cos, sin)


OUT_TM = 256


def _out_ln_kernel(ma_ref, mb_ref, w_ref, x_ref, g_ref, b_ref, o_ref):
    half = ma_ref.shape[1]
    y = jnp.dot(ma_ref[...], w_ref[:half, :], preferred_element_type=F32)
    y = y + jnp.dot(mb_ref[...], w_ref[half:, :], preferred_element_type=F32)
    z = DN_ALPHA * x_ref[...] + y
    mu = jnp.mean(z, axis=-1, keepdims=True)
    zc = z - mu
    var = jnp.mean(zc * zc, axis=-1, keepdims=True)
    o_ref[...] = zc * lax.rsqrt(var + LN_EPS) * g_ref[...] + b_ref[...]


def _out_ln(mix_a, a_blk, mix_b, b_blk, w_out, x2d, ln_g, ln_b):
    n_tok = x2d.shape[0]
    half = D_MODEL // 2
    return pl.pallas_call(
        _out_ln_kernel,
        grid=(n_tok // OUT_TM,),
        in_specs=[pl.BlockSpec((OUT_TM, half), lambda i: (i, a_blk)),
                  pl.BlockSpec((OUT_TM, half), lambda i: (i, b_blk)),
                  pl.BlockSpec((D_MODEL, D_MODEL), lambda i: (0, 0)),
                  pl.BlockSpec((OUT_TM, D_MODEL), lambda i: (i, 0)),
                  pl.BlockSpec((1, D_MODEL), lambda i: (0, 0)),
                  pl.BlockSpec((1, D_MODEL), lambda i: (0, 0))],
        out_specs=pl.BlockSpec((OUT_TM, D_MODEL), lambda i: (i, 0)),
        out_shape=jax.ShapeDtypeStruct((n_tok, D_MODEL), F32),
        compiler_params=_cparams(("parallel",)),
        name="out_ln",
    )(mix_a, mix_b, w_out, x2d, ln_g, ln_b)


def _s5_tables(a_re, a_im, log_dt, b_re, b_im, c_re, c_im, d_skip):
    hi = lax.Precision.HIGHEST
    ar = a_re.astype(F32)
    ai = a_im.astype(F32)
    dt = jnp.exp(log_dt.astype(F32))[..., None]

    def lam_pow(k):
        kk = k.astype(F32)[:, None, None, None]
        mag = jnp.exp(kk * (ar * dt)[None])
        ang = kk * (ai * dt)[None]
        return mag * jnp.cos(ang), mag * jnp.sin(ang)

    pw_re, pw_im = lam_pow(jnp.arange(S5_CHUNK + 1))
    lb_re, lb_im = pw_re[1], pw_im[1]
    nr = lb_re - 1.0
    den = jnp.square(ar) + jnp.square(ai)
    f_re = (nr * ar + lb_im * ai) / den
    f_im = (lb_im * ar - nr * ai) / den
    br = b_re.astype(F32)[None]
    bi = b_im.astype(F32)[None]
    bb_re = f_re[..., None] * br - f_im[..., None] * bi
    bb_im = f_re[..., None] * bi + f_im[..., None] * br
    cr = c_re.astype(F32)
    ci = c_im.astype(F32)

    L = S5_CHUNK
    eye8 = jnp.eye(8, dtype=F32)

    def w_in(direction, powers):
        p_re = pw_re[powers, direction]
        p_im = pw_im[powers, direction]
        wr = jnp.einsum('sgp,gpc->gscp', p_re, bb_re[direction]) - jnp.einsum('sgp,gpc->gscp', p_im, bb_im[direction])
        wi = jnp.einsum('sgp,gpc->gscp', p_re, bb_im[direction]) + jnp.einsum('sgp,gpc->gscp', p_im, bb_re[direction])
        w = jnp.stack([wr, wi], axis=3)
        w = w.reshape(S5_OCTETS, 8, L, SSM_GROUP, 2, SSM_STATE)
        w = jnp.einsum('agscrp,gh->asgcrhp', w, eye8)
        return w.reshape(S5_OCTETS, S5_FEAT, 2 * S5_HALF)

    s_idx = jnp.arange(L)
    w_in_all = jnp.concatenate([w_in(0, L - 1 - s_idx), w_in(1, s_idx)], axis=-1)

    def w_out(direction, powers):
        p_re = pw_re[powers, direction]
        p_im = pw_im[powers, direction]
        wr = jnp.einsum('gop,tgp->gpto', cr[direction], p_re) - jnp.einsum('gop,tgp->gpto', ci[direction], p_im)
        wi = -(jnp.einsum('gop,tgp->gpto', cr[direction], p_im) + jnp.einsum('gop,tgp->gpto', ci[direction], p_re))
        w = jnp.stack([wr, wi], axis=1)
        w = w.reshape(S5_OCTETS, 8, 2, SSM_STATE, L, SSM_GROUP)
        w = jnp.einsum('agrpto,gh->argptho', w, eye8)
        return w.reshape(S5_OCTETS, 2 * S5_HALF, S5_FEAT)

    w_out_all = jnp.concatenate([w_out(0, s_idx + 1), w_out(1, L - s_idx)], axis=1)

    def k_lag(direction):
        p_re = pw_re[:L, direction]
        p_im = pw_im[:L, direction]
        cl_re = jnp.einsum('gop,kgp->kgop', cr[direction], p_re) - jnp.einsum('gop,kgp->kgop', ci[direction], p_im)
        cl_im = jnp.einsum('gop,kgp->kgop', cr[direction], p_im) + jnp.einsum('gop,kgp->kgop', ci[direction], p_re)
        return (jnp.einsum('kgop,gpi->gkio', cl_re, bb_re[direction], precision=hi)
                - jnp.einsum('kgop,gpi->gkio', cl_im, bb_im[direction], precision=hi))

    kf = k_lag(0)
    kr = k_lag(1)
    lag = s_idx[None, :] - s_idx[:, None]
    fwd = jnp.where((lag >= 0)[None, :, :, None, None], kf[:, jnp.clip(lag, 0, L - 1)], 0.0)
    rev = jnp.where((lag <= 0)[None, :, :, None, None], kr[:, jnp.clip(-lag, 0, L - 1)], 0.0)
    skip = (jnp.eye(L, dtype=F32)[None, :, :, None, None]
            * (jnp.eye(SSM_GROUP, dtype=F32)[None] * d_skip.astype(F32).reshape(SSM_GROUPS, 1, SSM_GROUP))[:, None, None])
    w_x = (fwd + rev + skip).transpose(0, 1, 3, 2, 4)
    w_x = w_x.reshape(S5_OCTETS, 8, L, SSM_GROUP, L, SSM_GROUP)
    w_x = jnp.einsum('agsito,gh->asgitho', w_x, eye8).reshape(S5_OCTETS, S5_FEAT, S5_FEAT)

    q_re, q_im = lam_pow(L * jnp.arange(1, 9))
    rows = jnp.arange(8)

    def lanes(x):
        return x.reshape(x.shape[:-2] + (S5_OCTETS, S5_HALF))

    def hs_tab(direction):
        out = []
        for part in (q_re, q_im):
            for sh in (1, 2, 4):
                coef = lanes(part[sh - 1, direction])
                mask = (rows >= sh) if direction == 0 else (rows <= 7 - sh)
                out.append(jnp.where(mask[None, :, None], coef[:, None, :], 0.0))
        for part in (q_re, q_im):
            idx = rows if direction == 0 else 7 - rows
            out.append(lanes(part[idx, direction]).transpose(1, 0, 2))
        return out

    tab = jnp.stack(hs_tab(0) + hs_tab(1), axis=1)
    return w_in_all.astype(BF16), w_x.astype(BF16), w_out_all.astype(BF16), tab


def _s5_kernel(u_ref, wi_ref, wx_ref, wo_ref, tab_ref, y_ref, s_ref, *, seqs, nchunk):
    u = u_ref[0]
    s_ref[...] = jnp.dot(u, wi_ref[0], preferred_element_type=F32)
    nv = nchunk // SUBLANES
    row = lax.broadcasted_iota(jnp.int32, (SUBLANES, S5_HALF), 0)
    zero = jnp.zeros((SUBLANES, S5_HALF), F32)
    H = S5_HALF

    def bcast(x, r):
        return jnp.broadcast_to(x[r:r + 1, :], x.shape)

    def scan_block(raw_re, raw_im, nb_raw_re, nb_raw_im, nb_e_re, nb_e_im, tab0, forward):
        edge, src = (0, SUBLANES - 1) if forward else (SUBLANES - 1, 0)
        step = 1 if forward else SUBLANES - 1
        x_re = jnp.where(row == edge, bcast(nb_raw_re, src), pltpu.roll(raw_re, step, 0))
        x_im = jnp.where(row == edge, bcast(nb_raw_im, src), pltpu.roll(raw_im, step, 0))
        for k, sh in enumerate((1, 2, 4)):
            c_re = tab_ref[0, tab0 + k]
            c_im = tab_ref[0, tab0 + 3 + k]
            amt = sh if forward else SUBLANES - sh
            s_re = pltpu.roll(x_re, amt, 0)
            s_im = pltpu.roll(x_im, amt, 0)
            x_re, x_im = x_re + c_re * s_re - c_im * s_im, x_im + c_re * s_im + c_im * s_re
        p_re = tab_ref[0, tab0 + 6]
        p_im = tab_ref[0, tab0 + 7]
        k_re = bcast(nb_e_re, src)
        k_im = bcast(nb_e_im, src)
        return x_re + p_re * k_re - p_im * k_im, x_im + p_re * k_im + p_im * k_re

    for b in range(seqs):
        base = b * nchunk

        def body(v, carry, base=base):
            f_raw_re, f_raw_im, f_e_re, f_e_im, r_raw_re, r_raw_im, r_e_re, r_e_im = carry
            r0 = pl.multiple_of(base + v * SUBLANES, SUBLANES)
            raw_re = s_ref[pl.ds(r0, SUBLANES), 0:H]
            raw_im = s_ref[pl.ds(r0, SUBLANES), H:2 * H]
            e_re, e_im = scan_block(raw_re, raw_im, f_raw_re, f_raw_im, f_e_re, f_e_im, 0, True)
            s_ref[pl.ds(r0, SUBLANES), 0:H] = e_re
            s_ref[pl.ds(r0, SUBLANES), H:2 * H] = e_im
            r1 = pl.multiple_of(base + (nv - 1 - v) * SUBLANES, SUBLANES)
            rraw_re = s_ref[pl.ds(r1, SUBLANES), 2 * H:3 * H]
            rraw_im = s_ref[pl.ds(r1, SUBLANES), 3 * H:4 * H]
            g_re, g_im = scan_block(rraw_re, rraw_im, r_raw_re, r_raw_im, r_e_re, r_e_im, 8, False)
            s_ref[pl.ds(r1, SUBLANES), 2 * H:3 * H] = g_re
            s_ref[pl.ds(r1, SUBLANES), 3 * H:4 * H] = g_im
            return raw_re, raw_im, e_re, e_im, rraw_re, rraw_im, g_re, g_im

        lax.fori_loop(0, nv, body, (zero,) * 8)

    y = jnp.dot(u, wx_ref[0], preferred_element_type=F32)
    y = y + jnp.dot(s_ref[...].astype(BF16), wo_ref[0], preferred_element_type=F32)
    y_ref[0] = y.astype(y_ref.dtype)


def _s5_scan(au, w_in, w_x, w_out, tab, t_len):
    n_tok = au.shape[1]
    nchunk = t_len // S5_CHUNK
    seqs = S5_ROWS // nchunk
    rows = n_tok // S5_CHUNK
    u = au.reshape(S5_OCTETS, rows, S5_FEAT)
    wspec = pl.BlockSpec((1, S5_FEAT, S5_FEAT), lambda o, i: (o, 0, 0), pipeline_mode=pl.Buffered(1))
    y = pl.pallas_call(
        functools.partial(_s5_kernel, seqs=seqs, nchunk=nchunk),
        grid=(S5_OCTETS, rows // S5_ROWS),
        in_specs=[pl.BlockSpec((1, S5_ROWS, S5_FEAT), lambda o, i: (o, i, 0)),
                  wspec, wspec, wspec,
                  pl.BlockSpec((1, 16, SUBLANES, S5_HALF), lambda o, i: (o, 0, 0, 0))],
        out_specs=pl.BlockSpec((1, S5_ROWS, S5_FEAT), lambda o, i: (o, i, 0)),
        out_shape=jax.ShapeDtypeStruct((S5_OCTETS, rows, S5_FEAT), BF16),
        scratch_shapes=[pltpu.VMEM((S5_ROWS, 4 * S5_HALF), F32)],
        compiler_params=_cparams(("arbitrary", "arbitrary")),
        name="s5_scan",
    )(u, w_in, w_x, w_out, tab)
    return y.reshape(S5_OCTETS, n_tok, LANES)


GLU_TM = 512


def _glu_kernel(y_ref, ag_ref, w_ref, b_ref, o_ref):
    y = jnp.concatenate([y_ref[g] for g in range(S5_OCTETS)], axis=-1).astype(F32)
    yg = jax.nn.gelu(y)
    z = jnp.dot(yg.astype(BF16), w_ref[...], preferred_element_type=F32) + b_ref[...]
    ya = yg * jax.nn.sigmoid(z)
    ag = ag_ref[...].astype(F32)
    o_ref[...] = (ya * (ag * jax.nn.sigmoid(ag))).astype(o_ref.dtype)


def _glu(y_oct, h_rest, glu_w, glu_b):
    n_tok = y_oct.shape[1]
    return pl.pallas_call(
        _glu_kernel,
        grid=(n_tok // GLU_TM,),
        in_specs=[pl.BlockSpec((S5_OCTETS, GLU_TM, LANES), lambda i: (0, i, 0)),
                  pl.BlockSpec((GLU_TM, SSM_WIDTH), lambda i: (i, 0)),
                  pl.BlockSpec((SSM_WIDTH, SSM_WIDTH), lambda i: (0, 0)),
                  pl.BlockSpec((1, SSM_WIDTH), lambda i: (0, 0))],
        out_specs=pl.BlockSpec((GLU_TM, SSM_WIDTH), lambda i: (i, 0)),
        out_shape=jax.ShapeDtypeStruct((n_tok, SSM_WIDTH), BF16),
        compiler_params=_cparams(("parallel",)),
        name="s5_glu",
    )(y_oct, h_rest, glu_w, glu_b)


GQA_TQ = 128
GQA_TK = 512
GQA_Q_BLK0 = 2
GQA_BG_BLK0 = 5
GQA_K_BLK0 = 16
GQA_V_BLK0 = 18


def _gqa_kernel(q_ref, k_ref, v_ref, bg_ref, o_ref, qs_ref, m_ref, l_ref, acc_ref, *, t_len):
    tq = GQA_TQ
    for g in range(B_GROUP):
        qs_ref[g * tq:(g + 1) * tq, :] = q_ref[:, g * HEAD_DIM:(g + 1) * HEAD_DIM]
    m_ref[...] = jnp.full(m_ref.shape, -jnp.inf, F32)
    l_ref[...] = jnp.zeros(l_ref.shape, F32)
    acc_ref[...] = jnp.zeros(acc_ref.shape, F32)

    def body(j, _):
        k0 = pl.multiple_of(j * GQA_TK, GQA_TK)
        kj = k_ref[pl.ds(k0, GQA_TK), :]
        vj = v_ref[pl.ds(k0, GQA_TK), :]
        s = lax.dot_general(qs_ref[...], kj, (((1,), (1,)), ((), ())), preferred_element_type=F32)
        m_old = m_ref[...]
        m_new = jnp.maximum(m_old, jnp.max(s, axis=-1, keepdims=True))
        p = jnp.exp(s - m_new)
        alpha = jnp.exp(m_old - m_new)
        l_ref[...] = alpha * l_ref[...] + jnp.sum(p, axis=-1, keepdims=True)
        acc_ref[...] = alpha * acc_ref[...] + jnp.dot(p.astype(BF16), vj, preferred_element_type=F32)
        m_ref[...] = m_new
        return 0

    lax.fori_loop(0, t_len // GQA_TK, body, 0)
    o = acc_ref[...] / l_ref[...]
    for g in range(B_GROUP):
        sl = slice(g * HEAD_DIM, (g + 1) * HEAD_DIM)
        bg = bg_ref[:, sl].astype(F32)
        o_ref[:, sl] = (o[g * tq:(g + 1) * tq, :] * (bg * jax.nn.sigmoid(bg))).astype(o_ref.dtype)


def _gqa(h_rest, bsz, t_len):
    n_tok = h_rest.shape[0]
    nq = t_len // GQA_TQ
    rows = B_GROUP * GQA_TQ
    return pl.pallas_call(
        functools.partial(_gqa_kernel, t_len=t_len),
        grid=(bsz, B_KV_HEADS, nq),
        in_specs=[pl.BlockSpec((GQA_TQ, B_GROUP * HEAD_DIM), lambda b, h, i: (b * nq + i, GQA_Q_BLK0 + h)),
                  pl.BlockSpec((t_len, HEAD_DIM), lambda b, h, i: (b, GQA_K_BLK0 + h)),
                  pl.BlockSpec((t_len, HEAD_DIM), lambda b, h, i: (b, GQA_V_BLK0 + h)),
                  pl.BlockSpec((GQA_TQ, B_GROUP * HEAD_DIM), lambda b, h, i: (b * nq + i, GQA_BG_BLK0 + h))],
        out_specs=pl.BlockSpec((GQA_TQ, B_GROUP * HEAD_DIM), lambda b, h, i: (b * nq + i, h)),
        out_shape=jax.ShapeDtypeStruct((n_tok, ATT_WIDTH), BF16),
        scratch_shapes=[pltpu.VMEM((rows, HEAD_DIM), BF16),
                        pltpu.VMEM((rows, 1), F32),
                        pltpu.VMEM((rows, 1), F32),
                        pltpu.VMEM((rows, HEAD_DIM), F32)],
        compiler_params=_cparams(("parallel", "parallel", "arbitrary")),
        name="gqa_attn",
    )(h_rest, h_rest, h_rest, h_rest)


DIL_BLOCK = 128
DIL_WIN = DIL_BLOCK + 2 * C_HALF_SPAN
DIL_TQ_MAX = 512


def _dil_kernel(*refs, tq, n_sub, has_prev, is_last):
    it = iter(refs)
    q_ref = next(it)
    km_ref, kb_ref, ka_ref = next(it), next(it), next(it)
    vm_ref, vb_ref, va_ref = next(it), next(it), next(it)
    po_ref = next(it) if has_prev else None
    pl_ref = next(it) if has_prev else None
    g_ref = next(it) if is_last else None
    o_ref = next(it)
    lse_ref = None if is_last else next(it)
    kbuf, vbuf = next(it), next(it)

    j = pl.program_id(2)
    hs = C_HALF_SPAN
    kbuf[0:hs, :] = kb_ref[0]
    kbuf[hs:hs + tq, :] = km_ref[0]
    kbuf[hs + tq:, :] = ka_ref[0]
    vbuf[0:hs, :] = vb_ref[0]
    vbuf[hs:hs + tq, :] = vm_ref[0]
    vbuf[hs + tq:, :] = va_ref[0]

    qi = lax.broadcasted_iota(jnp.int32, (DIL_BLOCK, DIL_WIN), 0)
    ci = lax.broadcasted_iota(jnp.int32, (DIL_BLOCK, DIL_WIN), 1)
    rel = ci - qi
    band = (rel >= 0) & (rel <= 2 * hs)
    lane = lax.broadcasted_iota(jnp.int32, (DIL_BLOCK, LANES), 1)

    def block(i, _):
        r0 = pl.multiple_of(i * DIL_BLOCK, DIL_BLOCK)
        pos = ci + (j * tq + r0 - hs)
        mask = band & (pos >= 0) & (pos < n_sub)
        lse_tile = jnp.zeros((DIL_BLOCK, LANES), F32)
        prev_lse = pl_ref[0, pl.ds(r0, DIL_BLOCK), :] if has_prev else None
        for h in range(C_HEADS):
            sl = slice(h * HEAD_DIM, (h + 1) * HEAD_DIM)
            qb = q_ref[0, pl.ds(r0, DIL_BLOCK), sl]
            kw = kbuf[pl.ds(r0, DIL_WIN), sl]
            vw = vbuf[pl.ds(r0, DIL_WIN), sl]
            s = lax.dot_general(qb, kw, (((1,), (1,)), ((), ())), preferred_element_type=F32)
            s = jnp.where(mask, s, -jnp.inf)
            m = jnp.max(s, axis=-1, keepdims=True)
            p = jnp.exp(s - m)
            l = jnp.sum(p, axis=-1, keepdims=True)
            o = jnp.dot(p.astype(BF16), vw, preferred_element_type=F32) / l
            lse = m + jnp.log(l)
            if has_prev:
                lp = prev_lse[:, h:h + 1]
                mx = jnp.maximum(lp, lse)
                e1 = jnp.exp(lp - mx)
                e2 = jnp.exp(lse - mx)
                den = e1 + e2
                o = (e1 * po_ref[0, pl.ds(r0, DIL_BLOCK), sl].astype(F32) + e2 * o) / den
                lse = mx + jnp.log(den)
            if is_last:
                g = g_ref[0, pl.ds(r0, DIL_BLOCK), sl].astype(F32)
                o = o * (g * jax.nn.sigmoid(g))
            else:
                lse_tile = jnp.where(lane == h, lse, lse_tile)
            o_ref[0, pl.ds(r0, DIL_BLOCK), sl] = o.astype(o_ref.dtype)
        if not is_last:
            lse_ref[0, pl.ds(r0, DIL_BLOCK), :] = lse_tile
        return 0

    lax.fori_loop(0, tq // DIL_BLOCK, block, 0)


def _dil_pass(h1, bsz, t_len, d, prev):
    n_tok = h1.shape[0]
    n_sub = t_len // d
    tq = min(DIL_TQ_MAX, n_sub)
    nt = n_sub // tq
    hb = tq // C_HALF_SPAN
    last_halo = n_sub // C_HALF_SPAN - 1
    width = C_HEADS * HEAD_DIM
    is_last = d == C_DILATIONS[-1]
    has_prev = prev is not None
    hv = h1.reshape(bsz, n_sub, d * 4 * width)

    def main(sec):
        return pl.BlockSpec((1, tq, width), lambda b, r, j: (b, j, r * 4 + sec))

    def before(sec):
        return pl.BlockSpec((1, C_HALF_SPAN, width), lambda b, r, j: (b, jnp.maximum(j * hb - 1, 0), r * 4 + sec))

    def after(sec):
        return pl.BlockSpec((1, C_HALF_SPAN, width),
                            lambda b, r, j: (b, jnp.minimum((j + 1) * hb, last_halo), r * 4 + sec))

    in_specs = [main(0), main(1), before(1), after(1), main(2), before(2), after(2)]
    args = [hv] * 7
    if has_prev:
        po, plse = prev
        in_specs += [pl.BlockSpec((1, tq, width), lambda b, r, j: (b, j, r)),
                     pl.BlockSpec((1, tq, LANES), lambda b, r, j: (b, j, r))]
        args += [po.reshape(bsz, n_sub, d * width), plse.reshape(bsz, n_sub, d * LANES)]
    if is_last:
        in_specs.append(main(3))
        args.append(hv)
    o_spec = pl.BlockSpec((1, tq, width), lambda b, r, j: (b, j, r))
    o_shape = jax.ShapeDtypeStruct((bsz, n_sub, d * width), BF16)
    if is_last:
        out_specs, out_shape = o_spec, o_shape
    else:
        out_specs = [o_spec, pl.BlockSpec((1, tq, LANES), lambda b, r, j: (b, j, r))]
        out_shape = [o_shape, jax.ShapeDtypeStruct((bsz, n_sub, d * LANES), F32)]
    res = pl.pallas_call(
        functools.partial(_dil_kernel, tq=tq, n_sub=n_sub, has_prev=has_prev, is_last=is_last),
        grid=(bsz, d, nt),
        in_specs=in_specs,
        out_specs=out_specs,
        out_shape=out_shape,
        scratch_shapes=[pltpu.VMEM((tq + 2 * C_HALF_SPAN, width), BF16),
                        pltpu.VMEM((tq + 2 * C_HALF_SPAN, width), BF16)],
        compiler_params=_cparams(("parallel", "parallel", "arbitrary")),
        name=f"dilated_d{d}",
    )(*args)
    if is_last:
        return res.reshape(n_tok, width)
    return res[0].reshape(n_tok, width), res[1].reshape(n_tok, LANES)


def _trunk(x, params):
    bsz, t_len, _ = x.shape
    x2d = x.reshape(bsz * t_len, D_MODEL)
    p = params
    au = _proj_au(x2d, p["w_au"])
    h_rest = _proj0(x2d, p["w_rest"], p["ax_cos"][t_len], p["ax_sin"][t_len], p["qn_g"], p["kn_g"], t_len)
    y_oct = _s5_scan(au, p["s5_w_in"], p["s5_w_x"], p["s5_w_out"], p["s5_tab"], t_len)
    mix_a = _glu(y_oct, h_rest, p["glu_w"], p["glu_b"])
    mix_b = _gqa(h_rest, bsz, t_len)
    x1 = _out_ln(mix_a, 0, mix_b, 0, p["even_w_out"], x2d, p["ln_g0"], p["ln_b0"])
    h1 = _proj1(x1, p["odd_w_in"], p["pr_cos"][t_len], p["pr_sin"][t_len], t_len)
    prev = None
    for d in C_DILATIONS[:-1]:
        prev = _dil_pass(h1, bsz, t_len, d, prev)
    mix = _dil_pass(h1, bsz, t_len, C_DILATIONS[-1], prev)
    x2 = _out_ln(mix, 0, mix, 1, p["odd_w_out"], x1, p["ln_g1"], p["ln_b1"])
    return x2.reshape(bsz, t_len, D_MODEL)


def kernel(x_prompt, x_sample, even_w_in, even_w_out, ssm_a_re, ssm_a_im, ssm_log_dt, ssm_b_re, ssm_b_im, ssm_c_re, ssm_c_im, ssm_d, ssm_glu_w, ssm_glu_b, attn_q_norm, attn_k_norm, odd_w_in, odd_w_out, ln_g, ln_b):
    w_in, w_x, w_out, tab = _s5_tables(ssm_a_re[0], ssm_a_im[0], ssm_log_dt[0], ssm_b_re[0], ssm_b_im[0],
                                       ssm_c_re[0], ssm_c_im[0], ssm_d[0])
    lens = sorted({x_prompt.shape[1], x_sample.shape[1]})
    ax = {t: _axial_tables(t) for t in lens}
    pr = {t: _partial_tables(t) for t in lens}
    params = {
        "w_au": even_w_in[0][:, :SSM_WIDTH].astype(BF16),
        "w_rest": even_w_in[0][:, SSM_WIDTH:].astype(BF16),
        "even_w_out": even_w_out[0].astype(BF16),
        "odd_w_in": odd_w_in[0].astype(BF16),
        "odd_w_out": odd_w_out[0].astype(BF16),
        "glu_w": ssm_glu_w[0].astype(BF16),
        "glu_b": ssm_glu_b[0].astype(F32).reshape(1, SSM_WIDTH),
        "qn_g": attn_q_norm[0].astype(F32).reshape(1, HEAD_DIM),
        "kn_g": attn_k_norm[0].astype(F32).reshape(1, HEAD_DIM),
        "ln_g0": ln_g[0].astype(F32).reshape(1, D_MODEL),
        "ln_b0": ln_b[0].astype(F32).reshape(1, D_MODEL),
        "ln_g1": ln_g[1].astype(F32).reshape(1, D_MODEL),
        "ln_b1": ln_b[1].astype(F32).reshape(1, D_MODEL),
        "s5_w_in": w_in, "s5_w_x": w_x, "s5_w_out": w_out, "s5_tab": tab,
        "ax_cos": {t: ax[t][0] for t in lens}, "ax_sin": {t: ax[t][1] for t in lens},
        "pr_cos": {t: pr[t][0] for t in lens}, "pr_sin": {t: pr[t][1] for t in lens},
    }
    return (_trunk(x_prompt, params), _trunk(x_sample, params))
```

```python
import functools
import math

import jax
import jax.numpy as jnp
from jax import lax
from jax.experimental import pallas as pl
from jax.experimental.pallas import tpu as pltpu

F32 = jnp.float32
BF16 = jnp.bfloat16

D_MODEL = 2048
HEAD_DIM = 128
GRID_W = 64
SSM_WIDTH = 1024
SSM_GROUP = 16
SSM_GROUPS = 64
SSM_STATE = 64
ATT_WIDTH = 1024
B_KV_HEADS = 2
B_GROUP = 4
KV_WIDTH = B_KV_HEADS * HEAD_DIM
AXIAL_THETA = 10000.0
C_HEADS = 16
C_DILATIONS = (1, 4, 16)
C_HALF_SPAN = 64
ROPE_THETA = 500000.0
ROPE_DIMS = 32
DEPTH = 2
DN_ALPHA = (2 * DEPTH) ** 0.25
LN_EPS = 1e-5
QK_EPS = 1e-6
ATT_SCALE = HEAD_DIM ** -0.5
LOG2E = math.log2(math.e)

LANES = 128
SUBLANES = 8
VMEM_LIMIT_BYTES = 56 * 1024 * 1024

S5_CHUNK = 16
S5_OCTETS = SSM_GROUPS // 8
S5_FEAT = S5_CHUNK * LANES
S5_HALF = 8 * SSM_STATE
S5_ROWS = 512


def _cparams(sem):
    return pltpu.CompilerParams(dimension_semantics=sem, vmem_limit_bytes=VMEM_LIMIT_BYTES)


def _axial_tables(t_len):
    pos = jnp.arange(t_len, dtype=jnp.int32)
    row = (pos // GRID_W).astype(F32)
    col = (pos % GRID_W).astype(F32)
    n = HEAD_DIM // 2
    inv = AXIAL_THETA ** (-jnp.arange(0, n, 2, dtype=F32) / n)
    ar = row[:, None] * inv[None, :]
    ac = col[:, None] * inv[None, :]
    cos = jnp.concatenate([jnp.cos(ar), jnp.cos(ar), jnp.cos(ac), jnp.cos(ac)], axis=-1)
    sin = jnp.concatenate([-jnp.sin(ar), jnp.sin(ar), -jnp.sin(ac), jnp.sin(ac)], axis=-1)
    return cos, sin


def _partial_tables(t_len):
    pos = jnp.arange(t_len, dtype=F32)
    n = ROPE_DIMS
    inv = ROPE_THETA ** (-jnp.arange(0, n, 2, dtype=F32) / n)
    ang = pos[:, None] * inv[None, :]
    rest = HEAD_DIM - n
    cos = jnp.concatenate([jnp.cos(ang), jnp.cos(ang), jnp.ones((t_len, rest), F32)], axis=-1)
    sin = jnp.concatenate([-jnp.sin(ang), jnp.sin(ang), jnp.zeros((t_len, rest), F32)], axis=-1)
    return cos, sin


def _rotate_pairs(x, cos, sin, half, period):
    lane = lax.broadcasted_iota(jnp.int32, x.shape, x.ndim - 1)
    up = pltpu.roll(x, LANES - half, x.ndim - 1)
    dn = pltpu.roll(x, half, x.ndim - 1)
    partner = jnp.where((lane % period) < half, up, dn)
    return x * cos + partner * sin


PROJ_TM = 1024


def _proj_au_kernel(x_ref, w_ref, o_ref, scr_ref):
    res = jnp.dot(x_ref[...].astype(BF16), w_ref[...], preferred_element_type=F32)
    rows = o_ref.shape[1]
    for g in range(S5_OCTETS):
        scr_ref[...] = res[:, g * LANES:(g + 1) * LANES]
        for s in range(S5_CHUNK):
            o_ref[g, :, s * LANES:(s + 1) * LANES] = scr_ref[pl.ds(s, rows, stride=S5_CHUNK), :].astype(o_ref.dtype)


PROJ_AU_TM = 512


def _proj_au(x2d, w_au):
    n_tok = x2d.shape[0]
    tm = PROJ_AU_TM
    return pl.pallas_call(
        _proj_au_kernel,
        grid=(n_tok // tm,),
        in_specs=[pl.BlockSpec((tm, D_MODEL), lambda i: (i, 0)),
                  pl.BlockSpec((D_MODEL, SSM_WIDTH), lambda i: (0, 0))],
        out_specs=pl.BlockSpec((S5_OCTETS, tm // S5_CHUNK, S5_FEAT), lambda i: (0, i, 0)),
        out_shape=jax.ShapeDtypeStruct((S5_OCTETS, n_tok // S5_CHUNK, S5_FEAT), BF16),
        scratch_shapes=[pltpu.VMEM((tm, LANES), F32)],
        compiler_params=_cparams(("parallel",)),
        name="proj_au",
    )(x2d, w_au)


P0_TN = 512
P0_Q_BLOCKS = (2, 3)
P0_KV_BLOCK = 4


def _proj0_kernel(x_ref, w_ref, cos_ref, sin_ref, qg_ref, kg_ref, o_ref, xb_ref):
    j = pl.program_id(1)

    @pl.when(j == 0)
    def _():
        xb_ref[...] = x_ref[...].astype(BF16)

    res = jnp.dot(xb_ref[...], w_ref[...], preferred_element_type=F32)

    def norm_rot(x, gain, scale):
        ms = jnp.mean(x * x, axis=-1, keepdims=True)
        xn = x * lax.rsqrt(ms + QK_EPS) * gain
        return _rotate_pairs(xn, cos_ref[...], sin_ref[...], HEAD_DIM // 4, HEAD_DIM // 2) * scale

    is_q = (j == P0_Q_BLOCKS[0]) | (j == P0_Q_BLOCKS[1])
    is_kv = j == P0_KV_BLOCK

    @pl.when(is_q)
    def _():
        for h in range(P0_TN // HEAD_DIM):
            sl = slice(h * HEAD_DIM, (h + 1) * HEAD_DIM)
            o_ref[:, sl] = norm_rot(res[:, sl], qg_ref[...], ATT_SCALE * LOG2E).astype(o_ref.dtype)

    @pl.when(is_kv)
    def _():
        for h in range(B_KV_HEADS):
            sl = slice(h * HEAD_DIM, (h + 1) * HEAD_DIM)
            o_ref[:, sl] = norm_rot(res[:, sl], kg_ref[...], 1.0).astype(o_ref.dtype)
        o_ref[:, KV_WIDTH:] = res[:, KV_WIDTH:].astype(o_ref.dtype)

    @pl.when(jnp.logical_not(is_q | is_kv))
    def _():
        o_ref[...] = res.astype(o_ref.dtype)


def _proj0(x2d, w_rest, cos, sin, qn_g, kn_g, t_len):
    n_tok = x2d.shape[0]
    n_cols = w_rest.shape[1]
    tm = min(PROJ_TM, t_len)
    tpb = t_len // tm
    return pl.pallas_call(
        _proj0_kernel,
        grid=(n_tok // tm, n_cols // P0_TN),
        in_specs=[pl.BlockSpec((tm, D_MODEL), lambda i, j: (i, 0)),
                  pl.BlockSpec((D_MODEL, P0_TN), lambda i, j: (0, j)),
                  pl.BlockSpec((tm, HEAD_DIM), lambda i, j: (i % tpb, 0)),
                  pl.BlockSpec((tm, HEAD_DIM), lambda i, j: (i % tpb, 0)),
                  pl.BlockSpec((1, HEAD_DIM), lambda i, j: (0, 0)),
                  pl.BlockSpec((1, HEAD_DIM), lambda i, j: (0, 0))],
        out_specs=pl.BlockSpec((tm, P0_TN), lambda i, j: (i, j)),
        out_shape=jax.ShapeDtypeStruct((n_tok, n_cols), BF16),
        scratch_shapes=[pltpu.VMEM((tm, D_MODEL), BF16)],
        compiler_params=_cparams(("parallel", "arbitrary")),
        name="proj0",
    )(x2d, w_rest, cos, sin, qn_g, kn_g)


P1_TN = 1024
P1_ROT_BLOCKS = 4
P1_Q_BLOCKS = 2
P1_QKV_BLOCKS = 6
DIL_FAN = 4


def _proj1_kernel(x_ref, w_ref, cos_ref, sin_ref, o_ref, o4_ref, o16_ref, rs_ref):
    j = pl.program_id(1)
    res = jnp.dot(x_ref[...], w_ref[...], preferred_element_type=F32)

    heads = P1_TN // HEAD_DIM
    slabs = [slice(h * HEAD_DIM, (h + 1) * HEAD_DIM) for h in range(heads)]

    @pl.when(j < P1_ROT_BLOCKS)
    def _():
        scale = jnp.where(j < P1_Q_BLOCKS, ATT_SCALE * LOG2E, 1.0).astype(F32)
        for h, sl in enumerate(slabs):
            rot = _rotate_pairs(res[:, sl], cos_ref[...], sin_ref[...], ROPE_DIMS // 2, HEAD_DIM)
            rs_ref[h] = rot * scale

    @pl.when(j >= P1_ROT_BLOCKS)
    def _():
        for h, sl in enumerate(slabs):
            rs_ref[h] = res[:, sl]

    tm = rs_ref.shape[1]
    r4 = tm // DIL_FAN
    r16 = tm // (DIL_FAN * DIL_FAN)
    for h, sl in enumerate(slabs):
        o_ref[:, sl] = rs_ref[h].astype(o_ref.dtype)

    @pl.when(j < P1_QKV_BLOCKS)
    def _():
        for h, sl in enumerate(slabs):
            for rho in range(DIL_FAN):
                o4_ref[0, rho, :, sl] = rs_ref[h, pl.ds(rho, r4, stride=DIL_FAN), :].astype(o4_ref.dtype)

    for h, sl in enumerate(slabs):
        for rho in range(DIL_FAN):
            for kap in range(DIL_FAN):
                o16_ref[0, rho, kap, :, sl] = rs_ref[
                    h, pl.ds(DIL_FAN * kap + rho, r16, stride=DIL_FAN * DIL_FAN), :].astype(o16_ref.dtype)


def _proj1(xb2d, w, cos, sin, bsz, t_len):
    n_tok = xb2d.shape[0]
    n_cols = w.shape[1]
    tm = min(PROJ_TM, t_len)
    tpb = t_len // tm
    f = DIL_FAN
    qkv_cols = P1_QKV_BLOCKS * P1_TN
    return pl.pallas_call(
        _proj1_kernel,
        grid=(n_tok // tm, n_cols // P1_TN),
        in_specs=[pl.BlockSpec((tm, D_MODEL), lambda i, j: (i, 0)),
                  pl.BlockSpec((D_MODEL, P1_TN), lambda i, j: (0, j)),
                  pl.BlockSpec((tm, HEAD_DIM), lambda i, j: (i % tpb, 0)),
                  pl.BlockSpec((tm, HEAD_DIM), lambda i, j: (i % tpb, 0))],
        out_specs=[pl.BlockSpec((tm, P1_TN), lambda i, j: (i, j)),
                   pl.BlockSpec((1, f, tm // f, P1_TN),
                                lambda i, j: (i // tpb, 0, i % tpb, jnp.minimum(j, P1_QKV_BLOCKS - 1))),
                   pl.BlockSpec((1, f, f, tm // (f * f), P1_TN), lambda i, j: (i // tpb, 0, 0, i % tpb, j))],
        out_shape=[jax.ShapeDtypeStruct((n_tok, n_cols), BF16),
                   jax.ShapeDtypeStruct((bsz, f, t_len // f, qkv_cols), BF16),
                   jax.ShapeDtypeStruct((bsz, f, f, t_len // (f * f), n_cols), BF16)],
        scratch_shapes=[pltpu.VMEM((P1_TN // HEAD_DIM, tm, HEAD_DIM), F32)],
        compiler_params=_cparams(("parallel", "arbitrary")),
        name="proj1",
    )(xb2d, w, cos, sin)


OUT_TM = 256


def _residual_ln(x, y, g, b):
    z = DN_ALPHA * x + y
    mu = jnp.mean(z, axis=-1, keepdims=True)
    zc = z - mu
    var = jnp.mean(zc * zc, axis=-1, keepdims=True)
    return zc * lax.rsqrt(var + LN_EPS) * g + b


def _out_ln0_kernel(ma_ref, mb_ref, w_ref, x_ref, g_ref, b_ref, o_ref, ob_ref):
    half = ma_ref.shape[1]
    y = jnp.dot(ma_ref[...], w_ref[:half, :], preferred_element_type=F32)
    y = y + jnp.dot(mb_ref[...], w_ref[half:, :], preferred_element_type=F32)
    out = _residual_ln(x_ref[...], y, g_ref[...], b_ref[...])
    o_ref[...] = out
    ob_ref[...] = out.astype(ob_ref.dtype)


def _out_ln0(mix_a, mix_b, w_out, x2d, ln_g, ln_b):
    n_tok = x2d.shape[0]
    half = D_MODEL // 2
    row = lambda i: (i, 0)
    fixed = lambda i: (0, 0)
    return pl.pallas_call(
        _out_ln0_kernel,
        grid=(n_tok // OUT_TM,),
        in_specs=[pl.BlockSpec((OUT_TM, half), row),
                  pl.BlockSpec((OUT_TM, half), row),
                  pl.BlockSpec((D_MODEL, D_MODEL), fixed),
                  pl.BlockSpec((OUT_TM, D_MODEL), row),
                  pl.BlockSpec((1, D_MODEL), fixed),
                  pl.BlockSpec((1, D_MODEL), fixed)],
        out_specs=[pl.BlockSpec((OUT_TM, D_MODEL), row), pl.BlockSpec((OUT_TM, D_MODEL), row)],
        out_shape=[jax.ShapeDtypeStruct((n_tok, D_MODEL), F32), jax.ShapeDtypeStruct((n_tok, D_MODEL), BF16)],
        compiler_params=_cparams(("parallel",)),
        name="out_ln0",
    )(mix_a, mix_b, w_out, x2d, ln_g, ln_b)


OUT_CLASSES = DIL_FAN * DIL_FAN
OUT_PER_CLASS = OUT_TM // OUT_CLASSES


def _out_ln1_kernel(m_ref, w_ref, x_ref, g_ref, b_ref, o_ref, ms_ref, ys_ref):
    for idx in range(OUT_CLASSES):
        ms_ref[idx * OUT_PER_CLASS:(idx + 1) * OUT_PER_CLASS, :] = m_ref[0, idx]
    y = jnp.dot(ms_ref[...], w_ref[...], preferred_element_type=F32)
    n_slab = D_MODEL // LANES
    for idx in range(OUT_CLASSES):
        rho, kap = divmod(idx, DIL_FAN)
        for c in range(n_slab):
            ys_ref[c, pl.ds(DIL_FAN * kap + rho, OUT_PER_CLASS, stride=OUT_CLASSES), :] = (
                y[idx * OUT_PER_CLASS:(idx + 1) * OUT_PER_CLASS, c * LANES:(c + 1) * LANES])
    y_tok = jnp.concatenate([ys_ref[c] for c in range(n_slab)], axis=-1)
    o_ref[...] = _residual_ln(x_ref[...], y_tok, g_ref[...], b_ref[...])


def _out_ln1(mix16, w_out, x2d, ln_g, ln_b, bsz, t_len):
    n_tok = x2d.shape[0]
    tpb = t_len // OUT_TM
    fixed = lambda i: (0, 0)
    return pl.pallas_call(
        _out_ln1_kernel,
        grid=(n_tok // OUT_TM,),
        in_specs=[pl.BlockSpec((1, OUT_CLASSES, OUT_PER_CLASS, D_MODEL), lambda i: (i // tpb, 0, i % tpb, 0)),
                  pl.BlockSpec((D_MODEL, D_MODEL), fixed),
                  pl.BlockSpec((OUT_TM, D_MODEL), lambda i: (i, 0)),
                  pl.BlockSpec((1, D_MODEL), fixed),
                  pl.BlockSpec((1, D_MODEL), fixed)],
        out_specs=pl.BlockSpec((OUT_TM, D_MODEL), lambda i: (i, 0)),
        out_shape=jax.ShapeDtypeStruct((n_tok, D_MODEL), F32),
        scratch_shapes=[pltpu.VMEM((OUT_TM, D_MODEL), BF16), pltpu.VMEM((D_MODEL // LANES, OUT_TM, LANES), F32)],
        compiler_params=_cparams(("parallel",)),
        name="out_ln1",
    )(mix16, w_out, x2d, ln_g, ln_b)


def _s5_tables(a_re, a_im, log_dt, b_re, b_im, c_re, c_im, d_skip):
    hi = lax.Precision.HIGHEST
    ar = a_re.astype(F32)
    ai = a_im.astype(F32)
    dt = jnp.exp(log_dt.astype(F32))[..., None]

    def lam_pow(k):
        kk = k.astype(F32)[:, None, None, None]
        mag = jnp.exp(kk * (ar * dt)[None])
        ang = kk * (ai * dt)[None]
        return mag * jnp.cos(ang), mag * jnp.sin(ang)

    pw_re, pw_im = lam_pow(jnp.arange(S5_CHUNK + 1))
    lb_re, lb_im = pw_re[1], pw_im[1]
    nr = lb_re - 1.0
    den = jnp.square(ar) + jnp.square(ai)
    f_re = (nr * ar + lb_im * ai) / den
    f_im = (lb_im * ar - nr * ai) / den
    br = b_re.astype(F32)[None]
    bi = b_im.astype(F32)[None]
    bb_re = f_re[..., None] * br - f_im[..., None] * bi
    bb_im = f_re[..., None] * bi + f_im[..., None] * br
    cr = c_re.astype(F32)
    ci = c_im.astype(F32)

    L = S5_CHUNK
    eye8 = jnp.eye(8, dtype=F32)

    def w_in(direction, powers):
        p_re = pw_re[powers, direction]
        p_im = pw_im[powers, direction]
        wr = jnp.einsum('sgp,gpc->gscp', p_re, bb_re[direction]) - jnp.einsum('sgp,gpc->gscp', p_im, bb_im[direction])
        wi = jnp.einsum('sgp,gpc->gscp', p_re, bb_im[direction]) + jnp.einsum('sgp,gpc->gscp', p_im, bb_re[direction])
        w = jnp.stack([wr, wi], axis=3)
        w = w.reshape(S5_OCTETS, 8, L, SSM_GROUP, 2, SSM_STATE)
        w = jnp.einsum('agscrp,gh->asgcrhp', w, eye8)
        return w.reshape(S5_OCTETS, S5_FEAT, 2 * S5_HALF)

    s_idx = jnp.arange(L)
    w_in_all = jnp.concatenate([w_in(0, L - 1 - s_idx), w_in(1, s_idx)], axis=-1)

    def w_out(direction, powers):
        p_re = pw_re[powers, direction]
        p_im = pw_im[powers, direction]
        wr = jnp.einsum('gop,tgp->gpto', cr[direction], p_re) - jnp.einsum('gop,tgp->gpto', ci[direction], p_im)
        wi = -(jnp.einsum('gop,tgp->gpto', cr[direction], p_im) + jnp.einsum('gop,tgp->gpto', ci[direction], p_re))
        w = jnp.stack([wr, wi], axis=1)
        w = w.reshape(S5_OCTETS, 8, 2, SSM_STATE, L, SSM_GROUP)
        w = jnp.einsum('agrpto,gh->argptho', w, eye8)
        return w.reshape(S5_OCTETS, 2 * S5_HALF, S5_FEAT)

    w_out_all = jnp.concatenate([w_out(0, s_idx + 1), w_out(1, L - s_idx)], axis=1)

    def k_lag(direction):
        p_re = pw_re[:L, direction]
        p_im = pw_im[:L, direction]
        cl_re = jnp.einsum('gop,kgp->kgop', cr[direction], p_re) - jnp.einsum('gop,kgp->kgop', ci[direction], p_im)
        cl_im = jnp.einsum('gop,kgp->kgop', cr[direction], p_im) + jnp.einsum('gop,kgp->kgop', ci[direction], p_re)
        return (jnp.einsum('kgop,gpi->gkio', cl_re, bb_re[direction], precision=hi)
                - jnp.einsum('kgop,gpi->gkio', cl_im, bb_im[direction], precision=hi))

    kf = k_lag(0)
    kr = k_lag(1)
    lag = s_idx[None, :] - s_idx[:, None]
    fwd = jnp.where((lag >= 0)[None, :, :, None, None], kf[:, jnp.clip(lag, 0, L - 1)], 0.0)
    rev = jnp.where((lag <= 0)[None, :, :, None, None], kr[:, jnp.clip(-lag, 0, L - 1)], 0.0)
    skip = (jnp.eye(L, dtype=F32)[None, :, :, None, None]
            * (jnp.eye(SSM_GROUP, dtype=F32)[None] * d_skip.astype(F32).reshape(SSM_GROUPS, 1, SSM_GROUP))[:, None, None])
    w_x = (fwd + rev + skip).transpose(0, 1, 3, 2, 4)
    w_x = w_x.reshape(S5_OCTETS, 8, L, SSM_GROUP, L, SSM_GROUP)
    w_x = jnp.einsum('agsito,gh->asgitho', w_x, eye8).reshape(S5_OCTETS, S5_FEAT, S5_FEAT)

    q_re, q_im = lam_pow(L * jnp.arange(1, 9))
    rows = jnp.arange(8)

    def lanes(x):
        return x.reshape(x.shape[:-2] + (S5_OCTETS, S5_HALF))

    def hs_tab(direction):
        out = []
        for part in (q_re, q_im):
            for sh in (1, 2, 4):
                coef = lanes(part[sh - 1, direction])
                mask = (rows >= sh) if direction == 0 else (rows <= 7 - sh)
                out.append(jnp.where(mask[None, :, None], coef[:, None, :], 0.0))
        for part in (q_re, q_im):
            idx = rows if direction == 0 else 7 - rows
            out.append(lanes(part[idx, direction]).transpose(1, 0, 2))
        return out

    tab = jnp.stack(hs_tab(0) + hs_tab(1), axis=1)
    return w_in_all.astype(BF16), w_x.astype(BF16), w_out_all.astype(BF16), tab


def _s5_kernel(u_ref, wi_ref, wx_ref, wo_ref, tab_ref, y_ref, s_ref, *, seqs, nchunk):
    u = u_ref[0]
    s_ref[...] = jnp.dot(u, wi_ref[0], preferred_element_type=F32)
    nv = nchunk // SUBLANES
    row = lax.broadcasted_iota(jnp.int32, (SUBLANES, S5_HALF), 0)
    zero = jnp.zeros((SUBLANES, S5_HALF), F32)
    H = S5_HALF

    def bcast(x, r):
        return jnp.broadcast_to(x[r:r + 1, :], x.shape)

    def scan_block(raw_re, raw_im, nb_raw_re, nb_raw_im, nb_e_re, nb_e_im, tab0, forward):
        edge, src = (0, SUBLANES - 1) if forward else (SUBLANES - 1, 0)
        step = 1 if forward else SUBLANES - 1
        x_re = jnp.where(row == edge, bcast(nb_raw_re, src), pltpu.roll(raw_re, step, 0))
        x_im = jnp.where(row == edge, bcast(nb_raw_im, src), pltpu.roll(raw_im, step, 0))
        for k, sh in enumerate((1, 2, 4)):
            c_re = tab_ref[0, tab0 + k]
            c_im = tab_ref[0, tab0 + 3 + k]
            amt = sh if forward else SUBLANES - sh
            s_re = pltpu.roll(x_re, amt, 0)
            s_im = pltpu.roll(x_im, amt, 0)
            x_re, x_im = x_re + c_re * s_re - c_im * s_im, x_im + c_re * s_im + c_im * s_re
        p_re = tab_ref[0, tab0 + 6]
        p_im = tab_ref[0, tab0 + 7]
        k_re = bcast(nb_e_re, src)
        k_im = bcast(nb_e_im, src)
        return x_re + p_re * k_re - p_im * k_im, x_im + p_re * k_im + p_im * k_re

    for b in range(seqs):
        base = b * nchunk

        def body(v, carry, base=base):
            f_raw_re, f_raw_im, f_e_re, f_e_im, r_raw_re, r_raw_im, r_e_re, r_e_im = carry
            r0 = pl.multiple_of(base + v * SUBLANES, SUBLANES)
            raw_re = s_ref[pl.ds(r0, SUBLANES), 0:H]
            raw_im = s_ref[pl.ds(r0, SUBLANES), H:2 * H]
            e_re, e_im = scan_block(raw_re, raw_im, f_raw_re, f_raw_im, f_e_re, f_e_im, 0, True)
            s_ref[pl.ds(r0, SUBLANES), 0:H] = e_re
            s_ref[pl.ds(r0, SUBLANES), H:2 * H] = e_im
            r1 = pl.multiple_of(base + (nv - 1 - v) * SUBLANES, SUBLANES)
            rraw_re = s_ref[pl.ds(r1, SUBLANES), 2 * H:3 * H]
            rraw_im = s_ref[pl.ds(r1, SUBLANES), 3 * H:4 * H]
            g_re, g_im = scan_block(rraw_re, rraw_im, r_raw_re, r_raw_im, r_e_re, r_e_im, 8, False)
            s_ref[pl.ds(r1, SUBLANES), 2 * H:3 * H] = g_re
            s_ref[pl.ds(r1, SUBLANES), 3 * H:4 * H] = g_im
            return raw_re, raw_im, e_re, e_im, rraw_re, rraw_im, g_re, g_im

        lax.fori_loop(0, nv, body, (zero,) * 8)

    y = jnp.dot(u, wx_ref[0], preferred_element_type=F32)
    y = y + jnp.dot(s_ref[...].astype(BF16), wo_ref[0], preferred_element_type=F32)
    y_ref[0] = y.astype(y_ref.dtype)


def _s5_scan(u, w_in, w_x, w_out, tab, t_len):
    rows = u.shape[1]
    nchunk = t_len // S5_CHUNK
    seqs = S5_ROWS // nchunk
    wspec = pl.BlockSpec((1, S5_FEAT, S5_FEAT), lambda o, i: (o, 0, 0), pipeline_mode=pl.Buffered(1))
    return pl.pallas_call(
        functools.partial(_s5_kernel, seqs=seqs, nchunk=nchunk),
        grid=(S5_OCTETS, rows // S5_ROWS),
        in_specs=[pl.BlockSpec((1, S5_ROWS, S5_FEAT), lambda o, i: (o, i, 0)),
                  wspec, wspec, wspec,
                  pl.BlockSpec((1, 16, SUBLANES, S5_HALF), lambda o, i: (o, 0, 0, 0))],
        out_specs=pl.BlockSpec((1, S5_ROWS, S5_FEAT), lambda o, i: (o, i, 0)),
        out_shape=jax.ShapeDtypeStruct((S5_OCTETS, rows, S5_FEAT), BF16),
        scratch_shapes=[pltpu.VMEM((S5_ROWS, 4 * S5_HALF), F32)],
        compiler_params=_cparams(("arbitrary", "arbitrary")),
        name="s5_scan",
    )(u, w_in, w_x, w_out, tab)


GLU_TM = 512


def _glu_kernel(y_ref, ag_ref, w_ref, b_ref, o_ref, ys_ref):
    rows = y_ref.shape[1]
    for g in range(S5_OCTETS):
        for s in range(S5_CHUNK):
            ys_ref[g, pl.ds(s, rows, stride=S5_CHUNK), :] = y_ref[g, :, s * LANES:(s + 1) * LANES].astype(F32)
    y = jnp.concatenate([ys_ref[g] for g in range(S5_OCTETS)], axis=-1)
    yg = jax.nn.gelu(y)
    z = jnp.dot(yg.astype(BF16), w_ref[...], preferred_element_type=F32) + b_ref[...]
    ya = yg * jax.nn.sigmoid(z)
    ag = ag_ref[...].astype(F32)
    o_ref[...] = (ya * (ag * jax.nn.sigmoid(ag))).astype(o_ref.dtype)


def _glu(y_oct, h_rest, glu_w, glu_b):
    n_tok = y_oct.shape[1] * S5_CHUNK
    return pl.pallas_call(
        _glu_kernel,
        grid=(n_tok // GLU_TM,),
        in_specs=[pl.BlockSpec((S5_OCTETS, GLU_TM // S5_CHUNK, S5_FEAT), lambda i: (0, i, 0)),
                  pl.BlockSpec((GLU_TM, SSM_WIDTH), lambda i: (i, 0)),
                  pl.BlockSpec((SSM_WIDTH, SSM_WIDTH), lambda i: (0, 0)),
                  pl.BlockSpec((1, SSM_WIDTH), lambda i: (0, 0))],
        out_specs=pl.BlockSpec((GLU_TM, SSM_WIDTH), lambda i: (i, 0)),
        out_shape=jax.ShapeDtypeStruct((n_tok, SSM_WIDTH), BF16),
        scratch_shapes=[pltpu.VMEM((S5_OCTETS, GLU_TM, LANES), F32)],
        compiler_params=_cparams(("parallel",)),
        name="s5_glu",
    )(y_oct, h_rest, glu_w, glu_b)


GQA_TQ = 128
GQA_TK = 512
GQA_Q_BLK0 = 2
GQA_BG_BLK0 = 5
GQA_K_BLK0 = 16
GQA_V_BLK0 = 18


def _gqa_kernel(q_ref, k_ref, v_ref, bg_ref, o_ref, qs_ref, va_ref, m_ref, acc_ref, *, t_len):
    tq = GQA_TQ

    @pl.when(pl.program_id(2) == 0)
    def _():
        lane = lax.broadcasted_iota(jnp.int32, (t_len, HEAD_DIM), 1)
        va_ref[:, :HEAD_DIM] = v_ref[...]
        va_ref[:, HEAD_DIM:] = jnp.where(lane == 0, 1.0, 0.0).astype(BF16)

    for g in range(B_GROUP):
        qs_ref[g * tq:(g + 1) * tq, :] = q_ref[:, g * HEAD_DIM:(g + 1) * HEAD_DIM]
    m_ref[...] = jnp.full(m_ref.shape, -jnp.inf, F32)
    acc_ref[...] = jnp.zeros(acc_ref.shape, F32)

    def body(j, _):
        k0 = pl.multiple_of(j * GQA_TK, GQA_TK)
        kj = k_ref[pl.ds(k0, GQA_TK), :]
        vj = va_ref[pl.ds(k0, GQA_TK), :]
        s = lax.dot_general(qs_ref[...], kj, (((1,), (1,)), ((), ())), preferred_element_type=F32)
        m_old = m_ref[...]
        m_new = jnp.maximum(m_old, jnp.max(s, axis=-1, keepdims=True))
        p = jnp.exp2(s - pltpu.repeat(m_new, GQA_TK // LANES, 1))
        alpha = jnp.exp2(m_old - m_new)
        acc_ref[...] = (pltpu.repeat(alpha, 2, 1) * acc_ref[...]
                        + jnp.dot(p.astype(BF16), vj, preferred_element_type=F32))
        m_ref[...] = m_new
        return 0

    lax.fori_loop(0, t_len // GQA_TK, body, 0)
    acc = acc_ref[...]
    o = acc[:, :HEAD_DIM] / acc[:, HEAD_DIM:HEAD_DIM + 1]
    for g in range(B_GROUP):
        sl = slice(g * HEAD_DIM, (g + 1) * HEAD_DIM)
        bg = bg_ref[:, sl].astype(F32)
        o_ref[:, sl] = (o[g * tq:(g + 1) * tq, :] * (bg * jax.nn.sigmoid(bg))).astype(o_ref.dtype)


def _gqa(h_rest, bsz, t_len):
    n_tok = h_rest.shape[0]
    nq = t_len // GQA_TQ
    rows = B_GROUP * GQA_TQ
    return pl.pallas_call(
        functools.partial(_gqa_kernel, t_len=t_len),
        grid=(bsz, B_KV_HEADS, nq),
        in_specs=[pl.BlockSpec((GQA_TQ, B_GROUP * HEAD_DIM), lambda b, h, i: (b * nq + i, GQA_Q_BLK0 + h)),
                  pl.BlockSpec((t_len, HEAD_DIM), lambda b, h, i: (b, GQA_K_BLK0 + h)),
                  pl.BlockSpec((t_len, HEAD_DIM), lambda b, h, i: (b, GQA_V_BLK0 + h)),
                  pl.BlockSpec((GQA_TQ, B_GROUP * HEAD_DIM), lambda b, h, i: (b * nq + i, GQA_BG_BLK0 + h))],
        out_specs=pl.BlockSpec((GQA_TQ, B_GROUP * HEAD_DIM), lambda b, h, i: (b * nq + i, h)),
        out_shape=jax.ShapeDtypeStruct((n_tok, ATT_WIDTH), BF16),
        scratch_shapes=[pltpu.VMEM((rows, HEAD_DIM), BF16),
                        pltpu.VMEM((t_len, 2 * HEAD_DIM), BF16),
                        pltpu.VMEM((rows, LANES), F32),
                        pltpu.VMEM((rows, 2 * HEAD_DIM), F32)],
        compiler_params=_cparams(("parallel", "parallel", "arbitrary")),
        name="gqa_attn",
    )(h_rest, h_rest, h_rest, h_rest)


DIL_BLOCK = 128
DIL_WIN = DIL_BLOCK + 2 * C_HALF_SPAN
DIL_TQ_MAX = 512


def _dil_kernel(*refs, tq, n_sub, has_prev, is_last):
    it = iter(refs)
    q_ref = next(it)
    km_ref, kb_ref, ka_ref = next(it), next(it), next(it)
    vm_ref, vb_ref, va_ref = next(it), next(it), next(it)
    po_ref = next(it) if has_prev else None
    pl_ref = next(it) if has_prev else None
    g_ref = next(it) if is_last else None
    o_ref = next(it)
    lse_ref = None if is_last else next(it)
    kbuf, vbuf, acc_scr = next(it), next(it), next(it)
    oscr = None if is_last else next(it)
    lscr = None if is_last else next(it)

    j = pl.program_id(2)
    hs = C_HALF_SPAN
    kbuf[0:hs, :] = kb_ref[0, 0]
    kbuf[hs:hs + tq, :] = km_ref[0, 0]
    kbuf[hs + tq:, :] = ka_ref[0, 0]
    vbuf[0:hs, :] = vb_ref[0, 0]
    vbuf[hs:hs + tq, :] = vm_ref[0, 0]
    vbuf[hs + tq:, :] = va_ref[0, 0]

    qi = lax.broadcasted_iota(jnp.int32, (DIL_BLOCK, DIL_WIN), 0)
    ci = lax.broadcasted_iota(jnp.int32, (DIL_BLOCK, DIL_WIN), 1)
    rel = ci - qi
    band = (rel >= 0) & (rel <= 2 * hs)
    lane = lax.broadcasted_iota(jnp.int32, (DIL_BLOCK, LANES), 1)

    def block(i, _):
        r0 = pl.multiple_of(i * DIL_BLOCK, DIL_BLOCK)
        rows = pl.ds(r0, DIL_BLOCK)
        pos = ci + (j * tq + r0 - hs)
        mask = band & (pos >= 0) & (pos < n_sub)
        m_tile = jnp.zeros((DIL_BLOCK, LANES), F32)
        l_tile = jnp.ones((DIL_BLOCK, LANES), F32)
        for h in range(C_HEADS):
            sl = slice(h * HEAD_DIM, (h + 1) * HEAD_DIM)
            qb = q_ref[0, 0, rows, sl]
            kw = kbuf[pl.ds(r0, DIL_WIN), sl]
            vw = vbuf[pl.ds(r0, DIL_WIN), sl]
            s = lax.dot_general(qb, kw, (((1,), (1,)), ((), ())), preferred_element_type=F32)
            s = jnp.where(mask, s, -jnp.inf)
            m = jnp.max(s, axis=-1, keepdims=True)
            p = jnp.exp2(s - m)
            l = jnp.sum(p, axis=-1, keepdims=True)
            acc_scr[:, sl] = jnp.dot(p.astype(BF16), vw, preferred_element_type=F32)
            m_tile = jnp.where(lane == h, m, m_tile)
            l_tile = jnp.where(lane == h, l, l_tile)
        lse_t = m_tile + jnp.log2(l_tile)
        if has_prev:
            lp_t = pl_ref[0, 0, rows, :]
            mx = jnp.maximum(lp_t, lse_t)
            e1 = jnp.exp2(lp_t - mx)
            e2 = jnp.exp2(lse_t - mx)
            den = e1 + e2
            w_prev = e1 / den
            w_cur = e2 / (den * l_tile)
            lse_t = mx + jnp.log2(den)
        else:
            w_prev = None
            w_cur = 1.0 / l_tile
        for h in range(C_HEADS):
            sl = slice(h * HEAD_DIM, (h + 1) * HEAD_DIM)
            o = acc_scr[:, sl] * w_cur[:, h:h + 1]
            if has_prev:
                o = o + po_ref[0, 0, rows, sl].astype(F32) * w_prev[:, h:h + 1]
            if is_last:
                g = g_ref[0, 0, rows, sl].astype(F32)
                o_ref[0, 0, rows, sl] = (o * (g * jax.nn.sigmoid(g))).astype(o_ref.dtype)
            else:
                oscr[h, rows, :] = o
        if not is_last:
            lscr[rows, :] = lse_t
        return 0

    lax.fori_loop(0, tq // DIL_BLOCK, block, 0)
    if not is_last:
        sub = tq // DIL_FAN
        for kap in range(DIL_FAN):
            for h in range(C_HEADS):
                o_ref[0, 0, kap, :, h * HEAD_DIM:(h + 1) * HEAD_DIM] = (
                    oscr[h, pl.ds(kap, sub, stride=DIL_FAN), :].astype(o_ref.dtype))
            lse_ref[0, 0, kap] = lscr[pl.ds(kap, sub, stride=DIL_FAN), :]


def _dil_pass(src, bsz, t_len, d, prev):
    n_sub = t_len // d
    tq = min(DIL_TQ_MAX, n_sub)
    nt = n_sub // tq
    hs = C_HALF_SPAN
    hb = tq // hs
    last_halo = n_sub // hs - 1
    width = C_HEADS * HEAD_DIM
    f = DIL_FAN
    is_last = d == C_DILATIONS[-1]
    has_prev = prev is not None

    def main(sec, w=width):
        return pl.BlockSpec((1, 1, tq, w), lambda b, r, j: (b, r, j, sec))

    def before(sec):
        return pl.BlockSpec((1, 1, hs, width), lambda b, r, j: (b, r, jnp.maximum(j * hb - 1, 0), sec))

    def after(sec):
        return pl.BlockSpec((1, 1, hs, width), lambda b, r, j: (b, r, jnp.minimum((j + 1) * hb, last_halo), sec))

    in_specs = [main(0), main(1), before(1), after(1), main(2), before(2), after(2)]
    args = [src] * 7
    if has_prev:
        in_specs += [main(0), main(0, LANES)]
        args += list(prev)
    if is_last:
        in_specs.append(main(3))
        args.append(src)
    scratch = [pltpu.VMEM((tq + 2 * hs, width), BF16),
               pltpu.VMEM((tq + 2 * hs, width), BF16),
               pltpu.VMEM((DIL_BLOCK, width), F32)]
    if is_last:
        out_specs = main(0)
        out_shape = jax.ShapeDtypeStruct((bsz, d, n_sub, width), BF16)
    else:
        out_specs = [pl.BlockSpec((1, 1, f, tq // f, width), lambda b, r, j: (b, r, 0, j, 0)),
                     pl.BlockSpec((1, 1, f, tq // f, LANES), lambda b, r, j: (b, r, 0, j, 0))]
        out_shape = [jax.ShapeDtypeStruct((bsz, d, f, n_sub // f, width), BF16),
                     jax.ShapeDtypeStruct((bsz, d, f, n_sub // f, LANES), F32)]
        scratch += [pltpu.VMEM((C_HEADS, tq, HEAD_DIM), F32), pltpu.VMEM((tq, LANES), F32)]
    res = pl.pallas_call(
        functools.partial(_dil_kernel, tq=tq, n_sub=n_sub, has_prev=has_prev, is_last=is_last),
        grid=(bsz, d, nt),
        in_specs=in_specs,
        out_specs=out_specs,
        out_shape=out_shape,
        scratch_shapes=scratch,
        compiler_params=_cparams(("parallel", "parallel", "arbitrary")),
        name=f"dilated_d{d}",
    )(*args)
    if is_last:
        return res
    return (res[0].reshape(bsz, d * f, n_sub // f, width), res[1].reshape(bsz, d * f, n_sub // f, LANES))


def _trunk(x, params):
    bsz, t_len, _ = x.shape
    x2d = x.reshape(bsz * t_len, D_MODEL)
    p = params
    au = _proj_au(x2d, p["w_au"])
    h_rest = _proj0(x2d, p["w_rest"], p["ax_cos"][t_len], p["ax_sin"][t_len], p["qn_g"], p["kn_g"], t_len)
    y_oct = _s5_scan(au, p["s5_w_in"], p["s5_w_x"], p["s5_w_out"], p["s5_tab"], t_len)
    mix_a = _glu(y_oct, h_rest, p["glu_w"], p["glu_b"])
    mix_b = _gqa(h_rest, bsz, t_len)
    x1, x1b = _out_ln0(mix_a, mix_b, p["even_w_out"], x2d, p["ln_g0"], p["ln_b0"])
    h1, h4, h16 = _proj1(x1b, p["odd_w_in"], p["pr_cos"][t_len], p["pr_sin"][t_len], bsz, t_len)
    prev = _dil_pass(h1.reshape(bsz, 1, t_len, h1.shape[1]), bsz, t_len, 1, None)
    prev = _dil_pass(h4, bsz, t_len, DIL_FAN, prev)
    n16 = t_len // OUT_CLASSES
    mix16 = _dil_pass(h16.reshape(bsz, OUT_CLASSES, n16, h16.shape[-1]), bsz, t_len, OUT_CLASSES, prev)
    x2 = _out_ln1(mix16, p["odd_w_out"], x1, p["ln_g1"], p["ln_b1"], bsz, t_len)
    return x2.reshape(bsz, t_len, D_MODEL)


def kernel(x_prompt, x_sample, even_w_in, even_w_out, ssm_a_re, ssm_a_im, ssm_log_dt, ssm_b_re, ssm_b_im, ssm_c_re, ssm_c_im, ssm_d, ssm_glu_w, ssm_glu_b, attn_q_norm, attn_k_norm, odd_w_in, odd_w_out, ln_g, ln_b):
    w_in, w_x, w_out, tab = _s5_tables(ssm_a_re[0], ssm_a_im[0], ssm_log_dt[0], ssm_b_re[0], ssm_b_im[0],
                                       ssm_c_re[0], ssm_c_im[0], ssm_d[0])
    lens = sorted({x_prompt.shape[1], x_sample.shape[1]})
    ax = {t: _axial_tables(t) for t in lens}
    pr = {t: _partial_tables(t) for t in lens}
    params = {
        "w_au": even_w_in[0][:, :SSM_WIDTH].astype(BF16),
        "w_rest": even_w_in[0][:, SSM_WIDTH:].astype(BF16),
        "even_w_out": even_w_out[0].astype(BF16),
        "odd_w_in": odd_w_in[0].astype(BF16),
        "odd_w_out": odd_w_out[0].astype(BF16),
        "glu_w": ssm_glu_w[0].astype(BF16),
        "glu_b": ssm_glu_b[0].astype(F32).reshape(1, SSM_WIDTH),
        "qn_g": attn_q_norm[0].astype(F32).reshape(1, HEAD_DIM),
        "kn_g": attn_k_norm[0].astype(F32).reshape(1, HEAD_DIM),
        "ln_g0": ln_g[0].astype(F32).reshape(1, D_MODEL),
        "ln_b0": ln_b[0].astype(F32).reshape(1, D_MODEL),
        "ln_g1": ln_g[1].astype(F32).reshape(1, D_MODEL),
        "ln_b1": ln_b[1].astype(F32).reshape(1, D_MODEL),
        "s5_w_in": w_in, "s5_w_x": w_x, "s5_w_out": w_out, "s5_tab": tab,
        "ax_cos": {t: ax[t][0] for t in lens}, "ax_sin": {t: ax[t][1] for t in lens},
        "pr_cos": {t: pr[t][0] for t in lens}, "pr_sin": {t: pr[t][1] for t in lens},
    }
    return (_trunk(x_prompt, params), _trunk(x_sample, params))
```

```python
import functools
import math

import jax
import jax.numpy as jnp
from jax import lax
from jax.experimental import pallas as pl
from jax.experimental.pallas import tpu as pltpu

F32 = jnp.float32
BF16 = jnp.bfloat16

D_MODEL = 2048
HEAD_DIM = 128
GRID_W = 64
SSM_WIDTH = 1024
SSM_GROUP = 16
SSM_GROUPS = 64
SSM_STATE = 64
ATT_WIDTH = 1024
B_KV_HEADS = 2
B_GROUP = 4
KV_WIDTH = B_KV_HEADS * HEAD_DIM
AXIAL_THETA = 10000.0
C_HEADS = 16
C_DILATIONS = (1, 4, 16)
C_HALF_SPAN = 64
ROPE_THETA = 500000.0
ROPE_DIMS = 32
DEPTH = 2
DN_ALPHA = (2 * DEPTH) ** 0.25
LN_EPS = 1e-5
QK_EPS = 1e-6
ATT_SCALE = HEAD_DIM ** -0.5
LOG2E = math.log2(math.e)

LANES = 128
SUBLANES = 8
VMEM_LIMIT_BYTES = 56 * 1024 * 1024

S5_CHUNK = 16
S5_OCTETS = SSM_GROUPS // 8
S5_FEAT = S5_CHUNK * LANES
S5_HALF = 8 * SSM_STATE
S5_ROWS = 512


def _cparams(sem):
    return pltpu.CompilerParams(dimension_semantics=sem, vmem_limit_bytes=VMEM_LIMIT_BYTES)


def _axial_tables(t_len):
    pos = jnp.arange(t_len, dtype=jnp.int32)
    row = (pos // GRID_W).astype(F32)
    col = (pos % GRID_W).astype(F32)
    n = HEAD_DIM // 2
    inv = AXIAL_THETA ** (-jnp.arange(0, n, 2, dtype=F32) / n)
    ar = row[:, None] * inv[None, :]
    ac = col[:, None] * inv[None, :]
    cos = jnp.concatenate([jnp.cos(ar), jnp.cos(ar), jnp.cos(ac), jnp.cos(ac)], axis=-1)
    sin = jnp.concatenate([-jnp.sin(ar), jnp.sin(ar), -jnp.sin(ac), jnp.sin(ac)], axis=-1)
    return cos, sin


def _partial_tables(t_len):
    pos = jnp.arange(t_len, dtype=F32)
    n = ROPE_DIMS
    inv = ROPE_THETA ** (-jnp.arange(0, n, 2, dtype=F32) / n)
    ang = pos[:, None] * inv[None, :]
    rest = HEAD_DIM - n
    cos = jnp.concatenate([jnp.cos(ang), jnp.cos(ang), jnp.ones((t_len, rest), F32)], axis=-1)
    sin = jnp.concatenate([-jnp.sin(ang), jnp.sin(ang), jnp.zeros((t_len, rest), F32)], axis=-1)
    return cos, sin


def _rotate_pairs(x, cos, sin, half, period):
    lane = lax.broadcasted_iota(jnp.int32, x.shape, x.ndim - 1)
    up = pltpu.roll(x, LANES - half, x.ndim - 1)
    dn = pltpu.roll(x, half, x.ndim - 1)
    partner = jnp.where((lane % period) < half, up, dn)
    return x * cos + partner * sin


PROJ_TM = 1024


def _proj_au_kernel(x_ref, w_ref, o_ref, scr_ref):
    res = jnp.dot(x_ref[...].astype(BF16), w_ref[...], preferred_element_type=F32)
    rows = o_ref.shape[1]
    for g in range(S5_OCTETS):
        scr_ref[...] = res[:, g * LANES:(g + 1) * LANES]
        for s in range(S5_CHUNK):
            o_ref[g, :, s * LANES:(s + 1) * LANES] = scr_ref[pl.ds(s, rows, stride=S5_CHUNK), :].astype(o_ref.dtype)


PROJ_AU_TM = 512


def _proj_au(x2d, w_au):
    n_tok = x2d.shape[0]
    tm = PROJ_AU_TM
    return pl.pallas_call(
        _proj_au_kernel,
        grid=(n_tok // tm,),
        in_specs=[pl.BlockSpec((tm, D_MODEL), lambda i: (i, 0)),
                  pl.BlockSpec((D_MODEL, SSM_WIDTH), lambda i: (0, 0))],
        out_specs=pl.BlockSpec((S5_OCTETS, tm // S5_CHUNK, S5_FEAT), lambda i: (0, i, 0)),
        out_shape=jax.ShapeDtypeStruct((S5_OCTETS, n_tok // S5_CHUNK, S5_FEAT), BF16),
        scratch_shapes=[pltpu.VMEM((tm, LANES), F32)],
        compiler_params=_cparams(("parallel",)),
        name="proj_au",
    )(x2d, w_au)


P0_TN = 512
P0_Q_BLOCKS = (2, 3)
P0_KV_BLOCK = 4
P0_ROW_PIECES = 4


def _proj0_kernel(x_ref, w_ref, cos_ref, sin_ref, qg_ref, kg_ref, o_ref, xb_ref):
    j = pl.program_id(1)

    @pl.when(j == 0)
    def _():
        xb_ref[...] = x_ref[...].astype(BF16)

    is_q = (j == P0_Q_BLOCKS[0]) | (j == P0_Q_BLOCKS[1])
    is_kv = j == P0_KV_BLOCK
    gain = jnp.where(is_q, qg_ref[...], kg_ref[...])
    scale = jnp.where(is_q, ATT_SCALE * LOG2E, 1.0).astype(F32)
    tm = xb_ref.shape[0]
    piece = tm // P0_ROW_PIECES
    for r in range(P0_ROW_PIECES):
        rows = slice(r * piece, (r + 1) * piece)
        res = jnp.dot(xb_ref[rows, :], w_ref[...], preferred_element_type=F32)
        for h in range(P0_TN // HEAD_DIM):
            sl = slice(h * HEAD_DIM, (h + 1) * HEAD_DIM)
            normed = (is_q | is_kv) if h < B_KV_HEADS else is_q
            x = res[:, sl]
            ms = jnp.mean(x * x, axis=-1, keepdims=True)
            xn = x * jnp.where(normed, lax.rsqrt(ms + QK_EPS), 1.0) * jnp.where(normed, gain, 1.0)
            cos = jnp.where(normed, cos_ref[rows, :], 1.0)
            sin = jnp.where(normed, sin_ref[rows, :], 0.0)
            val = _rotate_pairs(xn, cos, sin, HEAD_DIM // 4, HEAD_DIM // 2) * scale
            o_ref[rows, sl] = val.astype(o_ref.dtype)


def _proj0(x2d, w_rest, cos, sin, qn_g, kn_g, t_len):
    n_tok = x2d.shape[0]
    n_cols = w_rest.shape[1]
    tm = min(PROJ_TM, t_len)
    tpb = t_len // tm
    return pl.pallas_call(
        _proj0_kernel,
        grid=(n_tok // tm, n_cols // P0_TN),
        in_specs=[pl.BlockSpec((tm, D_MODEL), lambda i, j: (i, 0)),
                  pl.BlockSpec((D_MODEL, P0_TN), lambda i, j: (0, j)),
                  pl.BlockSpec((tm, HEAD_DIM), lambda i, j: (i % tpb, 0)),
                  pl.BlockSpec((tm, HEAD_DIM), lambda i, j: (i % tpb, 0)),
                  pl.BlockSpec((1, HEAD_DIM), lambda i, j: (0, 0)),
                  pl.BlockSpec((1, HEAD_DIM), lambda i, j: (0, 0))],
        out_specs=pl.BlockSpec((tm, P0_TN), lambda i, j: (i, j)),
        out_shape=jax.ShapeDtypeStruct((n_tok, n_cols), BF16),
        scratch_shapes=[pltpu.VMEM((tm, D_MODEL), BF16)],
        compiler_params=_cparams(("parallel", "arbitrary")),
        name="proj0",
    )(x2d, w_rest, cos, sin, qn_g, kn_g)


P1_TN = 1024
P1_ROT_BLOCKS = 4
P1_Q_BLOCKS = 2
P1_ROW_PIECES = 4
DIL_FAN = 4


def _proj1_kernel(x_ref, w_ref, cos_ref, sin_ref, o_ref, o4_ref, o16_ref, rs_ref, r4_ref):
    j = pl.program_id(1)
    heads = P1_TN // HEAD_DIM
    slabs = [slice(h * HEAD_DIM, (h + 1) * HEAD_DIM) for h in range(heads)]

    rotated = j < P1_ROT_BLOCKS
    scale = jnp.where(j < P1_Q_BLOCKS, ATT_SCALE * LOG2E, 1.0).astype(F32)
    piece = rs_ref.shape[1] // P1_ROW_PIECES
    r4 = piece // DIL_FAN
    r16 = piece // (DIL_FAN * DIL_FAN)
    for r in range(P1_ROW_PIECES):
        rows = slice(r * piece, (r + 1) * piece)
        res = jnp.dot(x_ref[rows, :], w_ref[...], preferred_element_type=F32)
        cos = jnp.where(rotated, cos_ref[rows, :], 1.0)
        sin = jnp.where(rotated, sin_ref[rows, :], 0.0)
        for h, sl in enumerate(slabs):
            val = _rotate_pairs(res[:, sl], cos, sin, ROPE_DIMS // 2, HEAD_DIM) * scale
            o_ref[rows, sl] = val.astype(o_ref.dtype)
            rs_ref[h, rows, :] = val
            for rho in range(DIL_FAN):
                cls = rs_ref[h, pl.ds(r * piece + rho, r4, stride=DIL_FAN), :]
                r4_ref[h, r * piece + rho * r4:r * piece + (rho + 1) * r4, :] = cls
                o4_ref[0, rho, r * r4:(r + 1) * r4, sl] = cls.astype(o4_ref.dtype)
            for rho in range(DIL_FAN):
                for kap in range(DIL_FAN):
                    o16_ref[0, rho, kap, r * r16:(r + 1) * r16, sl] = r4_ref[
                        h, pl.ds(r * piece + rho * r4 + kap, r16, stride=DIL_FAN), :].astype(o16_ref.dtype)


def _proj1(xb2d, w, cos, sin, bsz, t_len):
    n_tok = xb2d.shape[0]
    n_cols = w.shape[1]
    tm = min(PROJ_TM, t_len)
    tpb = t_len // tm
    f = DIL_FAN
    return pl.pallas_call(
        _proj1_kernel,
        grid=(n_tok // tm, n_cols // P1_TN),
        in_specs=[pl.BlockSpec((tm, D_MODEL), lambda i, j: (i, 0)),
                  pl.BlockSpec((D_MODEL, P1_TN), lambda i, j: (0, j)),
                  pl.BlockSpec((tm, HEAD_DIM), lambda i, j: (i % tpb, 0)),
                  pl.BlockSpec((tm, HEAD_DIM), lambda i, j: (i % tpb, 0))],
        out_specs=[pl.BlockSpec((tm, P1_TN), lambda i, j: (i, j)),
                   pl.BlockSpec((1, f, tm // f, P1_TN), lambda i, j: (i // tpb, 0, i % tpb, j)),
                   pl.BlockSpec((1, f, f, tm // (f * f), P1_TN), lambda i, j: (i // tpb, 0, 0, i % tpb, j))],
        out_shape=[jax.ShapeDtypeStruct((n_tok, n_cols), BF16),
                   jax.ShapeDtypeStruct((bsz, f, t_len // f, n_cols), BF16),
                   jax.ShapeDtypeStruct((bsz, f, f, t_len // (f * f), n_cols), BF16)],
        scratch_shapes=[pltpu.VMEM((P1_TN // HEAD_DIM, tm, HEAD_DIM), F32),
                        pltpu.VMEM((P1_TN // HEAD_DIM, tm, HEAD_DIM), F32)],
        compiler_params=_cparams(("parallel", "arbitrary")),
        name="proj1",
    )(xb2d, w, cos, sin)


OUT_TM = 512
OUT_ROW_PIECES = 2


def _residual_ln(x, y, g, b):
    z = DN_ALPHA * x + y
    mu = jnp.mean(z, axis=-1, keepdims=True)
    zc = z - mu
    var = jnp.mean(zc * zc, axis=-1, keepdims=True)
    return zc * lax.rsqrt(var + LN_EPS) * g + b


def _out_ln0_kernel(ma_ref, mb_ref, w_ref, x_ref, g_ref, b_ref, o_ref, ob_ref):
    half = ma_ref.shape[1]
    piece = OUT_TM // OUT_ROW_PIECES
    for r in range(OUT_ROW_PIECES):
        rows = slice(r * piece, (r + 1) * piece)
        y = jnp.dot(ma_ref[rows, :], w_ref[:half, :], preferred_element_type=F32)
        y = y + jnp.dot(mb_ref[rows, :], w_ref[half:, :], preferred_element_type=F32)
        out = _residual_ln(x_ref[rows, :], y, g_ref[...], b_ref[...])
        o_ref[rows, :] = out
        ob_ref[rows, :] = out.astype(ob_ref.dtype)


def _out_ln0(mix_a, mix_b, w_out, x2d, ln_g, ln_b):
    n_tok = x2d.shape[0]
    half = D_MODEL // 2
    row = lambda i: (i, 0)
    fixed = lambda i: (0, 0)
    return pl.pallas_call(
        _out_ln0_kernel,
        grid=(n_tok // OUT_TM,),
        in_specs=[pl.BlockSpec((OUT_TM, half), row),
                  pl.BlockSpec((OUT_TM, half), row),
                  pl.BlockSpec((D_MODEL, D_MODEL), fixed),
                  pl.BlockSpec((OUT_TM, D_MODEL), row),
                  pl.BlockSpec((1, D_MODEL), fixed),
                  pl.BlockSpec((1, D_MODEL), fixed)],
        out_specs=[pl.BlockSpec((OUT_TM, D_MODEL), row), pl.BlockSpec((OUT_TM, D_MODEL), row)],
        out_shape=[jax.ShapeDtypeStruct((n_tok, D_MODEL), F32), jax.ShapeDtypeStruct((n_tok, D_MODEL), BF16)],
        compiler_params=_cparams(("parallel",)),
        name="out_ln0",
    )(mix_a, mix_b, w_out, x2d, ln_g, ln_b)


OUT_CLASSES = DIL_FAN * DIL_FAN
OUT_PER_CLASS = OUT_TM // OUT_CLASSES


def _out_ln1_kernel(m_ref, w_ref, x_ref, g_ref, b_ref, o_ref, ms_ref, ys_ref):
    n_slab = D_MODEL // LANES
    piece = OUT_TM // OUT_ROW_PIECES
    per = piece // OUT_CLASSES
    for r in range(OUT_ROW_PIECES):
        rows = slice(r * piece, (r + 1) * piece)
        for idx in range(OUT_CLASSES):
            ms_ref[r * piece + idx * per:r * piece + (idx + 1) * per, :] = m_ref[0, idx, r * per:(r + 1) * per, :]
        y = jnp.dot(ms_ref[rows, :], w_ref[...], preferred_element_type=F32)
        for idx in range(OUT_CLASSES):
            rho, kap = divmod(idx, DIL_FAN)
            for c in range(n_slab):
                ys_ref[c, pl.ds(r * piece + DIL_FAN * kap + rho, per, stride=OUT_CLASSES), :] = (
                    y[idx * per:(idx + 1) * per, c * LANES:(c + 1) * LANES])
        y_tok = jnp.concatenate([ys_ref[c, rows, :] for c in range(n_slab)], axis=-1)
        o_ref[rows, :] = _residual_ln(x_ref[rows, :], y_tok, g_ref[...], b_ref[...])


def _out_ln1(mix16, w_out, x2d, ln_g, ln_b, bsz, t_len):
    n_tok = x2d.shape[0]
    tpb = t_len // OUT_TM
    fixed = lambda i: (0, 0)
    return pl.pallas_call(
        _out_ln1_kernel,
        grid=(n_tok // OUT_TM,),
        in_specs=[pl.BlockSpec((1, OUT_CLASSES, OUT_PER_CLASS, D_MODEL), lambda i: (i // tpb, 0, i % tpb, 0)),
                  pl.BlockSpec((D_MODEL, D_MODEL), fixed),
                  pl.BlockSpec((OUT_TM, D_MODEL), lambda i: (i, 0)),
                  pl.BlockSpec((1, D_MODEL), fixed),
                  pl.BlockSpec((1, D_MODEL), fixed)],
        out_specs=pl.BlockSpec((OUT_TM, D_MODEL), lambda i: (i, 0)),
        out_shape=jax.ShapeDtypeStruct((n_tok, D_MODEL), F32),
        scratch_shapes=[pltpu.VMEM((OUT_TM, D_MODEL), BF16), pltpu.VMEM((D_MODEL // LANES, OUT_TM, LANES), F32)],
        compiler_params=_cparams(("parallel",)),
        name="out_ln1",
    )(mix16, w_out, x2d, ln_g, ln_b)


def _s5_tables(a_re, a_im, log_dt, b_re, b_im, c_re, c_im, d_skip):
    hi = lax.Precision.HIGHEST
    ar = a_re.astype(F32)
    ai = a_im.astype(F32)
    dt = jnp.exp(log_dt.astype(F32))[..., None]

    def lam_pow(k):
        kk = k.astype(F32)[:, None, None, None]
        mag = jnp.exp(kk * (ar * dt)[None])
        ang = kk * (ai * dt)[None]
        return mag * jnp.cos(ang), mag * jnp.sin(ang)

    pw_re, pw_im = lam_pow(jnp.arange(S5_CHUNK + 1))
    lb_re, lb_im = pw_re[1], pw_im[1]
    nr = lb_re - 1.0
    den = jnp.square(ar) + jnp.square(ai)
    f_re = (nr * ar + lb_im * ai) / den
    f_im = (lb_im * ar - nr * ai) / den
    br = b_re.astype(F32)[None]
    bi = b_im.astype(F32)[None]
    bb_re = f_re[..., None] * br - f_im[..., None] * bi
    bb_im = f_re[..., None] * bi + f_im[..., None] * br
    cr = c_re.astype(F32)
    ci = c_im.astype(F32)

    L = S5_CHUNK

    def block_diag8(compact):
        r = compact.shape[-2] // 8
        c = compact.shape[-1]
        same = (jnp.arange(8 * r)[:, None] // r) == (jnp.arange(8 * c)[None, :] // c)
        return jnp.where(same, jnp.tile(compact, (1,) * (compact.ndim - 1) + (8,)), 0.0)

    def w_in(direction, powers):
        p_re = pw_re[powers, direction]
        p_im = pw_im[powers, direction]
        wr = jnp.einsum('sgp,gpc->gscp', p_re, bb_re[direction]) - jnp.einsum('sgp,gpc->gscp', p_im, bb_im[direction])
        wi = jnp.einsum('sgp,gpc->gscp', p_re, bb_im[direction]) + jnp.einsum('sgp,gpc->gscp', p_im, bb_re[direction])
        w = jnp.stack([wr, wi], axis=3)
        w = w.reshape(S5_OCTETS, 8, L, SSM_GROUP, 2, SSM_STATE).transpose(0, 2, 4, 1, 3, 5)
        w = block_diag8(w.reshape(S5_OCTETS, L, 2, LANES, SSM_STATE))
        return w.transpose(0, 1, 3, 2, 4).reshape(S5_OCTETS, S5_FEAT, 2 * S5_HALF)

    s_idx = jnp.arange(L)
    w_in_all = jnp.concatenate([w_in(0, L - 1 - s_idx), w_in(1, s_idx)], axis=-1)

    def w_out(direction, powers):
        p_re = pw_re[powers, direction]
        p_im = pw_im[powers, direction]
        wr = jnp.einsum('gop,tgp->gpto', cr[direction], p_re) - jnp.einsum('gop,tgp->gpto', ci[direction], p_im)
        wi = -(jnp.einsum('gop,tgp->gpto', cr[direction], p_im) + jnp.einsum('gop,tgp->gpto', ci[direction], p_re))
        w = jnp.stack([wr, wi], axis=1)
        w = w.reshape(S5_OCTETS, 8, 2, SSM_STATE, L, SSM_GROUP).transpose(0, 2, 4, 1, 3, 5)
        w = block_diag8(w.reshape(S5_OCTETS, 2, L, S5_HALF, SSM_GROUP))
        return w.transpose(0, 1, 3, 2, 4).reshape(S5_OCTETS, 2 * S5_HALF, S5_FEAT)

    w_out_all = jnp.concatenate([w_out(0, s_idx + 1), w_out(1, L - s_idx)], axis=1)

    def k_lag(direction):
        p_re = pw_re[:L, direction]
        p_im = pw_im[:L, direction]
        cl_re = jnp.einsum('gop,kgp->kgop', cr[direction], p_re) - jnp.einsum('gop,kgp->kgop', ci[direction], p_im)
        cl_im = jnp.einsum('gop,kgp->kgop', cr[direction], p_im) + jnp.einsum('gop,kgp->kgop', ci[direction], p_re)
        return (jnp.einsum('kgop,gpi->gkio', cl_re, bb_re[direction], precision=hi)
                - jnp.einsum('kgop,gpi->gkio', cl_im, bb_im[direction], precision=hi))

    kf = k_lag(0)
    kr = k_lag(1)
    lag = s_idx[None, :] - s_idx[:, None]
    fwd = jnp.where((lag >= 0)[None, :, :, None, None], kf[:, jnp.clip(lag, 0, L - 1)], 0.0)
    rev = jnp.where((lag <= 0)[None, :, :, None, None], kr[:, jnp.clip(-lag, 0, L - 1)], 0.0)
    skip = (jnp.eye(L, dtype=F32)[None, :, :, None, None]
            * (jnp.eye(SSM_GROUP, dtype=F32)[None] * d_skip.astype(F32).reshape(SSM_GROUPS, 1, SSM_GROUP))[:, None, None])
    w_x = (fwd + rev + skip).reshape(S5_OCTETS, 8, L, L, SSM_GROUP, SSM_GROUP).transpose(0, 2, 3, 1, 4, 5)
    w_x = block_diag8(w_x.reshape(S5_OCTETS, L, L, LANES, SSM_GROUP))
    w_x = w_x.transpose(0, 1, 3, 2, 4).reshape(S5_OCTETS, S5_FEAT, S5_FEAT)

    q_re, q_im = lam_pow(L * jnp.arange(1, 9))
    rows = jnp.arange(8)

    def lanes(x):
        return x.reshape(x.shape[:-2] + (S5_OCTETS, S5_HALF))

    def hs_tab(direction):
        out = []
        for part in (q_re, q_im):
            for sh in (1, 2, 4):
                coef = lanes(part[sh - 1, direction])
                mask = (rows >= sh) if direction == 0 else (rows <= 7 - sh)
                out.append(jnp.where(mask[None, :, None], coef[:, None, :], 0.0))
        for part in (q_re, q_im):
            idx = rows if direction == 0 else 7 - rows
            out.append(lanes(part[idx, direction]).transpose(1, 0, 2))
        return out

    tab = jnp.stack(hs_tab(0) + hs_tab(1), axis=1)
    return w_in_all.astype(BF16), w_x.astype(BF16), w_out_all.astype(BF16), tab


def _s5_kernel(u_ref, wi_ref, wx_ref, wo_ref, tab_ref, y_ref, s_ref, *, seqs, nchunk):
    u = u_ref[0]
    s_ref[...] = jnp.dot(u, wi_ref[0], preferred_element_type=F32)
    nv = nchunk // SUBLANES
    row = lax.broadcasted_iota(jnp.int32, (SUBLANES, S5_HALF), 0)
    zero = jnp.zeros((SUBLANES, S5_HALF), F32)
    H = S5_HALF

    def bcast(x, r):
        return jnp.broadcast_to(x[r:r + 1, :], x.shape)

    def scan_block(raw_re, raw_im, nb_raw_re, nb_raw_im, nb_e_re, nb_e_im, tab0, forward):
        edge, src = (0, SUBLANES - 1) if forward else (SUBLANES - 1, 0)
        step = 1 if forward else SUBLANES - 1
        x_re = jnp.where(row == edge, bcast(nb_raw_re, src), pltpu.roll(raw_re, step, 0))
        x_im = jnp.where(row == edge, bcast(nb_raw_im, src), pltpu.roll(raw_im, step, 0))
        for k, sh in enumerate((1, 2, 4)):
            c_re = tab_ref[0, tab0 + k]
            c_im = tab_ref[0, tab0 + 3 + k]
            amt = sh if forward else SUBLANES - sh
            s_re = pltpu.roll(x_re, amt, 0)
            s_im = pltpu.roll(x_im, amt, 0)
            x_re, x_im = x_re + c_re * s_re - c_im * s_im, x_im + c_re * s_im + c_im * s_re
        p_re = tab_ref[0, tab0 + 6]
        p_im = tab_ref[0, tab0 + 7]
        k_re = bcast(nb_e_re, src)
        k_im = bcast(nb_e_im, src)
        return x_re + p_re * k_re - p_im * k_im, x_im + p_re * k_im + p_im * k_re

    for b in range(seqs):
        base = b * nchunk

        def body(v, carry, base=base):
            f_raw_re, f_raw_im, f_e_re, f_e_im, r_raw_re, r_raw_im, r_e_re, r_e_im = carry
            r0 = pl.multiple_of(base + v * SUBLANES, SUBLANES)
            raw_re = s_ref[pl.ds(r0, SUBLANES), 0:H]
            raw_im = s_ref[pl.ds(r0, SUBLANES), H:2 * H]
            e_re, e_im = scan_block(raw_re, raw_im, f_raw_re, f_raw_im, f_e_re, f_e_im, 0, True)
            s_ref[pl.ds(r0, SUBLANES), 0:H] = e_re
            s_ref[pl.ds(r0, SUBLANES), H:2 * H] = e_im
            r1 = pl.multiple_of(base + (nv - 1 - v) * SUBLANES, SUBLANES)
            rraw_re = s_ref[pl.ds(r1, SUBLANES), 2 * H:3 * H]
            rraw_im = s_ref[pl.ds(r1, SUBLANES), 3 * H:4 * H]
            g_re, g_im = scan_block(rraw_re, rraw_im, r_raw_re, r_raw_im, r_e_re, r_e_im, 8, False)
            s_ref[pl.ds(r1, SUBLANES), 2 * H:3 * H] = g_re
            s_ref[pl.ds(r1, SUBLANES), 3 * H:4 * H] = g_im
            return raw_re, raw_im, e_re, e_im, rraw_re, rraw_im, g_re, g_im

        lax.fori_loop(0, nv, body, (zero,) * 8)

    y = jnp.dot(u, wx_ref[0], preferred_element_type=F32)
    y = y + jnp.dot(s_ref[...].astype(BF16), wo_ref[0], preferred_element_type=F32)
    y_ref[0] = y.astype(y_ref.dtype)


def _s5_scan(u, w_in, w_x, w_out, tab, t_len):
    rows = u.shape[1]
    nchunk = t_len // S5_CHUNK
    seqs = S5_ROWS // nchunk
    wspec = pl.BlockSpec((1, S5_FEAT, S5_FEAT), lambda o, i: (o, 0, 0), pipeline_mode=pl.Buffered(1))
    return pl.pallas_call(
        functools.partial(_s5_kernel, seqs=seqs, nchunk=nchunk),
        grid=(S5_OCTETS, rows // S5_ROWS),
        in_specs=[pl.BlockSpec((1, S5_ROWS, S5_FEAT), lambda o, i: (o, i, 0)),
                  wspec, wspec, wspec,
                  pl.BlockSpec((1, 16, SUBLANES, S5_HALF), lambda o, i: (o, 0, 0, 0))],
        out_specs=pl.BlockSpec((1, S5_ROWS, S5_FEAT), lambda o, i: (o, i, 0)),
        out_shape=jax.ShapeDtypeStruct((S5_OCTETS, rows, S5_FEAT), BF16),
        scratch_shapes=[pltpu.VMEM((S5_ROWS, 4 * S5_HALF), F32)],
        compiler_params=_cparams(("arbitrary", "arbitrary")),
        name="s5_scan",
    )(u, w_in, w_x, w_out, tab)


GLU_TM = 512


def _glu_kernel(y_ref, ag_ref, w_ref, b_ref, o_ref, ys_ref):
    rows = y_ref.shape[1]
    for g in range(S5_OCTETS):
        for s in range(S5_CHUNK):
            ys_ref[g, pl.ds(s, rows, stride=S5_CHUNK), :] = y_ref[g, :, s * LANES:(s + 1) * LANES].astype(F32)
    y = jnp.concatenate([ys_ref[g] for g in range(S5_OCTETS)], axis=-1)
    yg = jax.nn.gelu(y)
    z = jnp.dot(yg.astype(BF16), w_ref[...], preferred_element_type=F32) + b_ref[...]
    ya = yg * jax.nn.sigmoid(z)
    ag = ag_ref[...].astype(F32)
    o_ref[...] = (ya * (ag * jax.nn.sigmoid(ag))).astype(o_ref.dtype)


def _glu(y_oct, h_rest, glu_w, glu_b):
    n_tok = y_oct.shape[1] * S5_CHUNK
    return pl.pallas_call(
        _glu_kernel,
        grid=(n_tok // GLU_TM,),
        in_specs=[pl.BlockSpec((S5_OCTETS, GLU_TM // S5_CHUNK, S5_FEAT), lambda i: (0, i, 0)),
                  pl.BlockSpec((GLU_TM, SSM_WIDTH), lambda i: (i, 0)),
                  pl.BlockSpec((SSM_WIDTH, SSM_WIDTH), lambda i: (0, 0)),
                  pl.BlockSpec((1, SSM_WIDTH), lambda i: (0, 0))],
        out_specs=pl.BlockSpec((GLU_TM, SSM_WIDTH), lambda i: (i, 0)),
        out_shape=jax.ShapeDtypeStruct((n_tok, SSM_WIDTH), BF16),
        scratch_shapes=[pltpu.VMEM((S5_OCTETS, GLU_TM, LANES), F32)],
        compiler_params=_cparams(("parallel",)),
        name="s5_glu",
    )(y_oct, h_rest, glu_w, glu_b)


GQA_TQ = 256
GQA_TK = 512
GQA_Q_BLK0 = 2
GQA_BG_BLK0 = 5
GQA_K_BLK0 = 16
GQA_V_BLK0 = 18


def _gqa_kernel(q_ref, k_ref, v_ref, bg_ref, o_ref, qs_ref, kt_ref, va_ref, m_ref, acc_ref, *, t_len):
    tq = GQA_TQ
    n_chunks = t_len // GQA_TK

    @pl.when(pl.program_id(2) == 0)
    def _():
        lane = lax.broadcasted_iota(jnp.int32, (t_len, HEAD_DIM), 1)
        va_ref[:, :HEAD_DIM] = v_ref[...]
        va_ref[:, HEAD_DIM:] = jnp.where(lane == 0, 1.0, 0.0).astype(BF16)
        for c in range(n_chunks):
            cols = slice(c * GQA_TK, (c + 1) * GQA_TK)
            kt_ref[:, cols] = k_ref[cols, :].T

    for g in range(B_GROUP):
        qs_ref[g * tq:(g + 1) * tq, :] = q_ref[:, g * HEAD_DIM:(g + 1) * HEAD_DIM]
    m_ref[...] = jnp.full(m_ref.shape, -jnp.inf, F32)
    acc_ref[...] = jnp.zeros(acc_ref.shape, F32)

    for c in range(n_chunks):
        cols = slice(c * GQA_TK, (c + 1) * GQA_TK)
        s = jnp.dot(qs_ref[...], kt_ref[:, cols], preferred_element_type=F32)
        m_old = m_ref[...]
        m_new = jnp.maximum(m_old, jnp.max(s, axis=-1, keepdims=True))
        p = jnp.exp2(s - jnp.concatenate([m_new] * (GQA_TK // LANES), axis=1))
        alpha = jnp.exp2(m_old - m_new)
        acc_ref[...] = (jnp.concatenate([alpha, alpha], axis=1) * acc_ref[...]
                        + jnp.dot(p.astype(BF16), va_ref[cols, :], preferred_element_type=F32))
        m_ref[...] = m_new

    acc = acc_ref[...]
    o = acc[:, :HEAD_DIM] / acc[:, HEAD_DIM:HEAD_DIM + 1]
    for g in range(B_GROUP):
        sl = slice(g * HEAD_DIM, (g + 1) * HEAD_DIM)
        bg = bg_ref[:, sl].astype(F32)
        o_ref[:, sl] = (o[g * tq:(g + 1) * tq, :] * (bg * jax.nn.sigmoid(bg))).astype(o_ref.dtype)


def _gqa(h_rest, bsz, t_len):
    n_tok = h_rest.shape[0]
    nq = t_len // GQA_TQ
    rows = B_GROUP * GQA_TQ
    return pl.pallas_call(
        functools.partial(_gqa_kernel, t_len=t_len),
        grid=(bsz, B_KV_HEADS, nq),
        in_specs=[pl.BlockSpec((GQA_TQ, B_GROUP * HEAD_DIM), lambda b, h, i: (b * nq + i, GQA_Q_BLK0 + h)),
                  pl.BlockSpec((t_len, HEAD_DIM), lambda b, h, i: (b, GQA_K_BLK0 + h)),
                  pl.BlockSpec((t_len, HEAD_DIM), lambda b, h, i: (b, GQA_V_BLK0 + h)),
                  pl.BlockSpec((GQA_TQ, B_GROUP * HEAD_DIM), lambda b, h, i: (b * nq + i, GQA_BG_BLK0 + h))],
        out_specs=pl.BlockSpec((GQA_TQ, B_GROUP * HEAD_DIM), lambda b, h, i: (b * nq + i, h)),
        out_shape=jax.ShapeDtypeStruct((n_tok, ATT_WIDTH), BF16),
        scratch_shapes=[pltpu.VMEM((rows, HEAD_DIM), BF16),
                        pltpu.VMEM((HEAD_DIM, t_len), BF16),
                        pltpu.VMEM((t_len, 2 * HEAD_DIM), BF16),
                        pltpu.VMEM((rows, LANES), F32),
                        pltpu.VMEM((rows, 2 * HEAD_DIM), F32)],
        compiler_params=_cparams(("parallel", "parallel", "arbitrary")),
        name="gqa_attn",
    )(h_rest, h_rest, h_rest, h_rest)


DIL_BLOCK = 128
DIL_WIN = DIL_BLOCK + 2 * C_HALF_SPAN
DIL_TQ_MAX = 512


def _dil_kernel(*refs, tq, n_sub, has_prev, is_last):
    it = iter(refs)
    q_ref = next(it)
    km_ref, kb_ref, ka_ref = next(it), next(it), next(it)
    vm_ref, vb_ref, va_ref = next(it), next(it), next(it)
    po_ref = next(it) if has_prev else None
    pl_ref = next(it) if has_prev else None
    g_ref = next(it) if is_last else None
    o_ref = next(it)
    lse_ref = None if is_last else next(it)
    acc_scr = next(it)
    oscr = None if is_last else next(it)
    lscr = None if is_last else next(it)

    j = pl.program_id(2)
    hs = C_HALF_SPAN
    qi = lax.broadcasted_iota(jnp.int32, (DIL_BLOCK, DIL_WIN), 0)
    ci = lax.broadcasted_iota(jnp.int32, (DIL_BLOCK, DIL_WIN), 1)
    rel = ci - qi
    band = (rel >= 0) & (rel <= 2 * hs)
    lane = lax.broadcasted_iota(jnp.int32, (DIL_BLOCK, LANES), 1)
    ones = jnp.ones((DIL_WIN, HEAD_DIM), BF16)

    def window(main_ref, before_ref, after_ref, r0, sl):
        lo, hi = r0 - hs, r0 + DIL_BLOCK + hs
        parts = []
        if lo < 0:
            parts.append(before_ref[0, 0, hs + lo:hs, sl])
        parts.append(main_ref[0, 0, max(lo, 0):min(hi, tq), sl])
        if hi > tq:
            parts.append(after_ref[0, 0, 0:hi - tq, sl])
        return parts[0] if len(parts) == 1 else jnp.concatenate(parts, axis=0)

    for i in range(tq // DIL_BLOCK):
        r0 = i * DIL_BLOCK
        rows = slice(r0, r0 + DIL_BLOCK)
        pos = ci + (j * tq + r0 - hs)
        mask = band & (pos >= 0) & (pos < n_sub)
        m_tile = jnp.zeros((DIL_BLOCK, LANES), F32)
        l_tile = jnp.ones((DIL_BLOCK, LANES), F32)
        for h in range(C_HEADS):
            sl = slice(h * HEAD_DIM, (h + 1) * HEAD_DIM)
            qb = q_ref[0, 0, rows, sl]
            kw = window(km_ref, kb_ref, ka_ref, r0, sl)
            vw = jnp.concatenate([window(vm_ref, vb_ref, va_ref, r0, sl), ones], axis=1)
            s = lax.dot_general(qb, kw, (((1,), (1,)), ((), ())), preferred_element_type=F32)
            s = jnp.where(mask, s, -jnp.inf)
            m = jnp.max(s, axis=-1, keepdims=True)
            p = jnp.exp2(s - m)
            pv = jnp.dot(p.astype(BF16), vw, preferred_element_type=F32)
            acc_scr[rows, sl] = pv[:, :HEAD_DIM]
            m_tile = jnp.where(lane == h, m, m_tile)
            l_tile = jnp.where(lane == h, pv[:, HEAD_DIM:], l_tile)
        lse_t = m_tile + jnp.log2(l_tile)
        if has_prev:
            lp_t = pl_ref[0, 0, rows, :]
            mx = jnp.maximum(lp_t, lse_t)
            e1 = jnp.exp2(lp_t - mx)
            e2 = jnp.exp2(lse_t - mx)
            den = e1 + e2
            w_prev = e1 / den
            w_cur = e2 / (den * l_tile)
            lse_t = mx + jnp.log2(den)
        else:
            w_prev = None
            w_cur = 1.0 / l_tile
        for h in range(C_HEADS):
            sl = slice(h * HEAD_DIM, (h + 1) * HEAD_DIM)
            o = acc_scr[rows, sl] * w_cur[:, h:h + 1]
            if has_prev:
                o = o + po_ref[0, 0, rows, sl].astype(F32) * w_prev[:, h:h + 1]
            if is_last:
                g = g_ref[0, 0, rows, sl].astype(F32)
                o_ref[0, 0, rows, sl] = (o * (g * jax.nn.sigmoid(g))).astype(o_ref.dtype)
            else:
                oscr[h, rows, :] = o
        if not is_last:
            lscr[rows, :] = lse_t

    if not is_last:
        sub = tq // DIL_FAN
        for kap in range(DIL_FAN):
            for h in range(C_HEADS):
                o_ref[0, 0, kap, :, h * HEAD_DIM:(h + 1) * HEAD_DIM] = (
                    oscr[h, pl.ds(kap, sub, stride=DIL_FAN), :].astype(o_ref.dtype))
            lse_ref[0, 0, kap] = lscr[pl.ds(kap, sub, stride=DIL_FAN), :]


def _dil_pass(src, bsz, t_len, d, prev):
    n_sub = t_len // d
    tq = min(DIL_TQ_MAX, n_sub)
    nt = n_sub // tq
    hs = C_HALF_SPAN
    hb = tq // hs
    last_halo = n_sub // hs - 1
    width = C_HEADS * HEAD_DIM
    f = DIL_FAN
    is_last = d == C_DILATIONS[-1]
    has_prev = prev is not None

    def main(sec, w=width):
        return pl.BlockSpec((1, 1, tq, w), lambda b, r, j: (b, r, j, sec))

    def before(sec):
        return pl.BlockSpec((1, 1, hs, width), lambda b, r, j: (b, r, jnp.maximum(j * hb - 1, 0), sec))

    def after(sec):
        return pl.BlockSpec((1, 1, hs, width), lambda b, r, j: (b, r, jnp.minimum((j + 1) * hb, last_halo), sec))

    in_specs = [main(0), main(1), before(1), after(1), main(2), before(2), after(2)]
    args = [src] * 7
    if has_prev:
        in_specs += [main(0), main(0, LANES)]
        args += list(prev)
    if is_last:
        in_specs.append(main(3))
        args.append(src)
    scratch = [pltpu.VMEM((tq, width), F32)]
    if is_last:
        out_specs = main(0)
        out_shape = jax.ShapeDtypeStruct((bsz, d, n_sub, width), BF16)
    else:
        out_specs = [pl.BlockSpec((1, 1, f, tq // f, width), lambda b, r, j: (b, r, 0, j, 0)),
                     pl.BlockSpec((1, 1, f, tq // f, LANES), lambda b, r, j: (b, r, 0, j, 0))]
        out_shape = [jax.ShapeDtypeStruct((bsz, d, f, n_sub // f, width), BF16),
                     jax.ShapeDtypeStruct((bsz, d, f, n_sub // f, LANES), F32)]
        scratch += [pltpu.VMEM((C_HEADS, tq, HEAD_DIM), F32), pltpu.VMEM((tq, LANES), F32)]
    res = pl.pallas_call(
        functools.partial(_dil_kernel, tq=tq, n_sub=n_sub, has_prev=has_prev, is_last=is_last),
        grid=(bsz, d, nt),
        in_specs=in_specs,
        out_specs=out_specs,
        out_shape=out_shape,
        scratch_shapes=scratch,
        compiler_params=_cparams(("parallel", "parallel", "arbitrary")),
        name=f"dilated_d{d}",
    )(*args)
    if is_last:
        return res
    return (res[0].reshape(bsz, d * f, n_sub // f, width), res[1].reshape(bsz, d * f, n_sub // f, LANES))


def _trunk(x, params):
    bsz, t_len, _ = x.shape
    x2d = x.reshape(bsz * t_len, D_MODEL)
    p = params
    au = _proj_au(x2d, p["w_au"])
    h_rest = _proj0(x2d, p["w_rest"], p["ax_cos"][t_len], p["ax_sin"][t_len], p["qn_g"], p["kn_g"], t_len)
    y_oct = _s5_scan(au, p["s5_w_in"], p["s5_w_x"], p["s5_w_out"], p["s5_tab"], t_len)
    mix_a = _glu(y_oct, h_rest, p["glu_w"], p["glu_b"])
    mix_b = _gqa(h_rest, bsz, t_len)
    x1, x1b = _out_ln0(mix_a, mix_b, p["even_w_out"], x2d, p["ln_g0"], p["ln_b0"])
    h1, h4, h16 = _proj1(x1b, p["odd_w_in"], p["pr_cos"][t_len], p["pr_sin"][t_len], bsz, t_len)
    prev = _dil_pass(h1.reshape(bsz, 1, t_len, h1.shape[1]), bsz, t_len, 1, None)
    prev = _dil_pass(h4, bsz, t_len, DIL_FAN, prev)
    n16 = t_len // OUT_CLASSES
    mix16 = _dil_pass(h16.reshape(bsz, OUT_CLASSES, n16, h16.shape[-1]), bsz, t_len, OUT_CLASSES, prev)
    x2 = _out_ln1(mix16, p["odd_w_out"], x1, p["ln_g1"], p["ln_b1"], bsz, t_len)
    return x2.reshape(bsz, t_len, D_MODEL)


def kernel(x_prompt, x_sample, even_w_in, even_w_out, ssm_a_re, ssm_a_im, ssm_log_dt, ssm_b_re, ssm_b_im, ssm_c_re, ssm_c_im, ssm_d, ssm_glu_w, ssm_glu_b, attn_q_norm, attn_k_norm, odd_w_in, odd_w_out, ln_g, ln_b):
    w_in, w_x, w_out, tab = _s5_tables(ssm_a_re[0], ssm_a_im[0], ssm_log_dt[0], ssm_b_re[0], ssm_b_im[0],
                                       ssm_c_re[0], ssm_c_im[0], ssm_d[0])
    lens = sorted({x_prompt.shape[1], x_sample.shape[1]})
    ax = {t: _axial_tables(t) for t in lens}
    pr = {t: _partial_tables(t) for t in lens}
    params = {
        "w_au": even_w_in[0][:, :SSM_WIDTH].astype(BF16),
        "w_rest": even_w_in[0][:, SSM_WIDTH:].astype(BF16),
        "even_w_out": even_w_out[0].astype(BF16),
        "odd_w_in": odd_w_in[0].astype(BF16),
        "odd_w_out": odd_w_out[0].astype(BF16),
        "glu_w": ssm_glu_w[0].astype(BF16),
        "glu_b": ssm_glu_b[0].astype(F32).reshape(1, SSM_WIDTH),
        "qn_g": attn_q_norm[0].astype(F32).reshape(1, HEAD_DIM),
        "kn_g": attn_k_norm[0].astype(F32).reshape(1, HEAD_DIM),
        "ln_g0": ln_g[0].astype(F32).reshape(1, D_MODEL),
        "ln_b0": ln_b[0].astype(F32).reshape(1, D_MODEL),
        "ln_g1": ln_g[1].astype(F32).reshape(1, D_MODEL),
        "ln_b1": ln_b[1].astype(F32).reshape(1, D_MODEL),
        "s5_w_in": w_in, "s5_w_x": w_x, "s5_w_out": w_out, "s5_tab": tab,
        "ax_cos": {t: ax[t][0] for t in lens}, "ax_sin": {t: ax[t][1] for t in lens},
        "pr_cos": {t: pr[t][0] for t in lens}, "pr_sin": {t: pr[t][1] for t in lens},
    }
    return (_trunk(x_prompt, params), _trunk(x_sample, params))
```

```python
import functools
import math

import jax
import jax.numpy as jnp
from jax import lax
from jax.experimental import pallas as pl
from jax.experimental.pallas import tpu as pltpu

F32 = jnp.float32
BF16 = jnp.bfloat16

D_MODEL = 2048
HEAD_DIM = 128
GRID_W = 64
SSM_WIDTH = 1024
SSM_GROUP = 16
SSM_GROUPS = 64
SSM_STATE = 64
ATT_WIDTH = 1024
B_KV_HEADS = 2
B_GROUP = 4
KV_WIDTH = B_KV_HEADS * HEAD_DIM
AXIAL_THETA = 10000.0
C_HEADS = 16
C_DILATIONS = (1, 4, 16)
C_HALF_SPAN = 64
ROPE_THETA = 500000.0
ROPE_DIMS = 32
DEPTH = 2
DN_ALPHA = (2 * DEPTH) ** 0.25
LN_EPS = 1e-5
QK_EPS = 1e-6
ATT_SCALE = HEAD_DIM ** -0.5
LOG2E = math.log2(math.e)

LANES = 128
SUBLANES = 8
VMEM_LIMIT_BYTES = 56 * 1024 * 1024

S5_CHUNK = 16
S5_OCTETS = SSM_GROUPS // 8
S5_FEAT = S5_CHUNK * LANES
S5_HALF = 8 * SSM_STATE
S5_COMPACT = S5_FEAT // 8
S5_ROWS = 512


def _cparams(sem):
    return pltpu.CompilerParams(dimension_semantics=sem, vmem_limit_bytes=VMEM_LIMIT_BYTES)


def _axial_tables(t_len):
    pos = jnp.arange(t_len, dtype=jnp.int32)
    row = (pos // GRID_W).astype(F32)
    col = (pos % GRID_W).astype(F32)
    n = HEAD_DIM // 2
    inv = AXIAL_THETA ** (-jnp.arange(0, n, 2, dtype=F32) / n)
    ar = row[:, None] * inv[None, :]
    ac = col[:, None] * inv[None, :]
    cos = jnp.concatenate([jnp.cos(ar), jnp.cos(ar), jnp.cos(ac), jnp.cos(ac)], axis=-1)
    sin = jnp.concatenate([-jnp.sin(ar), jnp.sin(ar), -jnp.sin(ac), jnp.sin(ac)], axis=-1)
    return cos, sin


def _partial_tables(t_len):
    pos = jnp.arange(t_len, dtype=F32)
    n = ROPE_DIMS
    inv = ROPE_THETA ** (-jnp.arange(0, n, 2, dtype=F32) / n)
    ang = pos[:, None] * inv[None, :]
    rest = HEAD_DIM - n
    cos = jnp.concatenate([jnp.cos(ang), jnp.cos(ang), jnp.ones((t_len, rest), F32)], axis=-1)
    sin = jnp.concatenate([-jnp.sin(ang), jnp.sin(ang), jnp.zeros((t_len, rest), F32)], axis=-1)
    return cos, sin


def _rotate_pairs(x, cos, sin, half, period):
    lane = lax.broadcasted_iota(jnp.int32, x.shape, x.ndim - 1)
    up = pltpu.roll(x, LANES - half, x.ndim - 1)
    dn = pltpu.roll(x, half, x.ndim - 1)
    partner = jnp.where((lane % period) < half, up, dn)
    return x * cos + partner * sin


PROJ_TM = 1024


def _proj_au_kernel(x_ref, w_ref, o_ref, scr_ref):
    res = jnp.dot(x_ref[...].astype(BF16), w_ref[...], preferred_element_type=F32)
    rows = o_ref.shape[1]
    for g in range(S5_OCTETS):
        scr_ref[...] = res[:, g * LANES:(g + 1) * LANES]
        for s in range(S5_CHUNK):
            o_ref[g, :, s * LANES:(s + 1) * LANES] = scr_ref[pl.ds(s, rows, stride=S5_CHUNK), :].astype(o_ref.dtype)


PROJ_AU_TM = 512


def _proj_au(x2d, w_au):
    n_tok = x2d.shape[0]
    tm = PROJ_AU_TM
    return pl.pallas_call(
        _proj_au_kernel,
        grid=(n_tok // tm,),
        in_specs=[pl.BlockSpec((tm, D_MODEL), lambda i: (i, 0)),
                  pl.BlockSpec((D_MODEL, SSM_WIDTH), lambda i: (0, 0))],
        out_specs=pl.BlockSpec((S5_OCTETS, tm // S5_CHUNK, S5_FEAT), lambda i: (0, i, 0)),
        out_shape=jax.ShapeDtypeStruct((S5_OCTETS, n_tok // S5_CHUNK, S5_FEAT), BF16),
        scratch_shapes=[pltpu.VMEM((tm, LANES), F32)],
        compiler_params=_cparams(("parallel",)),
        name="proj_au",
    )(x2d, w_au)


P0_TN = 512
P0_Q_BLOCKS = (2, 3)
P0_KV_BLOCK = 4
P0_ROW_PIECES = 4


def _proj0_kernel(x_ref, w_ref, cos_ref, sin_ref, qg_ref, kg_ref, o_ref, xb_ref):
    j = pl.program_id(1)

    @pl.when(j == 0)
    def _():
        xb_ref[...] = x_ref[...].astype(BF16)

    is_q = (j == P0_Q_BLOCKS[0]) | (j == P0_Q_BLOCKS[1])
    is_kv = j == P0_KV_BLOCK
    gain = jnp.where(is_q, qg_ref[...], kg_ref[...])
    scale = jnp.where(is_q, ATT_SCALE * LOG2E, 1.0).astype(F32)
    tm = xb_ref.shape[0]
    piece = tm // P0_ROW_PIECES
    for r in range(P0_ROW_PIECES):
        rows = slice(r * piece, (r + 1) * piece)
        res = jnp.dot(xb_ref[rows, :], w_ref[...], preferred_element_type=F32)
        for h in range(P0_TN // HEAD_DIM):
            sl = slice(h * HEAD_DIM, (h + 1) * HEAD_DIM)
            normed = (is_q | is_kv) if h < B_KV_HEADS else is_q
            x = res[:, sl]
            ms = jnp.mean(x * x, axis=-1, keepdims=True)
            xn = x * jnp.where(normed, lax.rsqrt(ms + QK_EPS), 1.0) * jnp.where(normed, gain, 1.0)
            cos = jnp.where(normed, cos_ref[rows, :], 1.0)
            sin = jnp.where(normed, sin_ref[rows, :], 0.0)
            val = _rotate_pairs(xn, cos, sin, HEAD_DIM // 4, HEAD_DIM // 2) * scale
            o_ref[rows, sl] = val.astype(o_ref.dtype)


def _proj0(x2d, w_rest, cos, sin, qn_g, kn_g, t_len):
    n_tok = x2d.shape[0]
    n_cols = w_rest.shape[1]
    tm = min(PROJ_TM, t_len)
    tpb = t_len // tm
    return pl.pallas_call(
        _proj0_kernel,
        grid=(n_tok // tm, n_cols // P0_TN),
        in_specs=[pl.BlockSpec((tm, D_MODEL), lambda i, j: (i, 0)),
                  pl.BlockSpec((D_MODEL, P0_TN), lambda i, j: (0, j)),
                  pl.BlockSpec((tm, HEAD_DIM), lambda i, j: (i % tpb, 0)),
                  pl.BlockSpec((tm, HEAD_DIM), lambda i, j: (i % tpb, 0)),
                  pl.BlockSpec((1, HEAD_DIM), lambda i, j: (0, 0)),
                  pl.BlockSpec((1, HEAD_DIM), lambda i, j: (0, 0))],
        out_specs=pl.BlockSpec((tm, P0_TN), lambda i, j: (i, j)),
        out_shape=jax.ShapeDtypeStruct((n_tok, n_cols), BF16),
        scratch_shapes=[pltpu.VMEM((tm, D_MODEL), BF16)],
        compiler_params=_cparams(("parallel", "arbitrary")),
        name="proj0",
    )(x2d, w_rest, cos, sin, qn_g, kn_g)


P1_TM = 2048
P1_TN = 512
P1_ROT_BLOCKS = 2 * (C_HEADS * HEAD_DIM) // P1_TN
P1_Q_BLOCKS = (C_HEADS * HEAD_DIM) // P1_TN
P1_ROW_PIECES = 4
DIL_FAN = 4


def _proj1_kernel(x_ref, w_ref, cos_ref, sin_ref, o_ref, o4_ref, o16_ref, rs_ref, r4_ref):
    j = pl.program_id(1)
    heads = P1_TN // HEAD_DIM
    slabs = [slice(h * HEAD_DIM, (h + 1) * HEAD_DIM) for h in range(heads)]

    rotated = j < P1_ROT_BLOCKS
    scale = jnp.where(j < P1_Q_BLOCKS, ATT_SCALE * LOG2E, 1.0).astype(F32)
    piece = rs_ref.shape[1] // P1_ROW_PIECES
    r4 = piece // DIL_FAN
    r16 = piece // (DIL_FAN * DIL_FAN)
    for r in range(P1_ROW_PIECES):
        rows = slice(r * piece, (r + 1) * piece)
        res = jnp.dot(x_ref[rows, :], w_ref[...], preferred_element_type=F32)
        cos = jnp.where(rotated, cos_ref[rows, :], 1.0)
        sin = jnp.where(rotated, sin_ref[rows, :], 0.0)
        for h, sl in enumerate(slabs):
            val = _rotate_pairs(res[:, sl], cos, sin, ROPE_DIMS // 2, HEAD_DIM) * scale
            o_ref[rows, sl] = val.astype(o_ref.dtype)
            rs_ref[h, rows, :] = val
            for rho in range(DIL_FAN):
                cls = rs_ref[h, pl.ds(r * piece + rho, r4, stride=DIL_FAN), :]
                r4_ref[h, r * piece + rho * r4:r * piece + (rho + 1) * r4, :] = cls
                o4_ref[0, rho, r * r4:(r + 1) * r4, sl] = cls.astype(o4_ref.dtype)
            for rho in range(DIL_FAN):
                for kap in range(DIL_FAN):
                    o16_ref[0, rho, kap, r * r16:(r + 1) * r16, sl] = r4_ref[
                        h, pl.ds(r * piece + rho * r4 + kap, r16, stride=DIL_FAN), :].astype(o16_ref.dtype)


def _proj1(xb2d, w, cos, sin, bsz, t_len):
    n_tok = xb2d.shape[0]
    n_cols = w.shape[1]
    tm = min(P1_TM, t_len)
    tpb = t_len // tm
    f = DIL_FAN
    return pl.pallas_call(
        _proj1_kernel,
        grid=(n_tok // tm, n_cols // P1_TN),
        in_specs=[pl.BlockSpec((tm, D_MODEL), lambda i, j: (i, 0)),
                  pl.BlockSpec((D_MODEL, P1_TN), lambda i, j: (0, j)),
                  pl.BlockSpec((tm, HEAD_DIM), lambda i, j: (i % tpb, 0)),
                  pl.BlockSpec((tm, HEAD_DIM), lambda i, j: (i % tpb, 0))],
        out_specs=[pl.BlockSpec((tm, P1_TN), lambda i, j: (i, j)),
                   pl.BlockSpec((1, f, tm // f, P1_TN), lambda i, j: (i // tpb, 0, i % tpb, j)),
                   pl.BlockSpec((1, f, f, tm // (f * f), P1_TN), lambda i, j: (i // tpb, 0, 0, i % tpb, j))],
        out_shape=[jax.ShapeDtypeStruct((n_tok, n_cols), BF16),
                   jax.ShapeDtypeStruct((bsz, f, t_len // f, n_cols), BF16),
                   jax.ShapeDtypeStruct((bsz, f, f, t_len // (f * f), n_cols), BF16)],
        scratch_shapes=[pltpu.VMEM((P1_TN // HEAD_DIM, tm, HEAD_DIM), F32),
                        pltpu.VMEM((P1_TN // HEAD_DIM, tm, HEAD_DIM), F32)],
        compiler_params=_cparams(("parallel", "arbitrary")),
        name="proj1",
    )(xb2d, w, cos, sin)


OUT_TM = 512
OUT_ROW_PIECES = 2


def _residual_ln(x, y, g, b):
    z = DN_ALPHA * x + y
    mu = jnp.mean(z, axis=-1, keepdims=True)
    zc = z - mu
    var = jnp.mean(zc * zc, axis=-1, keepdims=True)
    return zc * lax.rsqrt(var + LN_EPS) * g + b


def _out_ln0_kernel(ma_ref, mb_ref, w_ref, x_ref, g_ref, b_ref, o_ref, ob_ref):
    half = ma_ref.shape[1]
    piece = OUT_TM // OUT_ROW_PIECES
    for r in range(OUT_ROW_PIECES):
        rows = slice(r * piece, (r + 1) * piece)
        y = jnp.dot(ma_ref[rows, :], w_ref[:half, :], preferred_element_type=F32)
        y = y + jnp.dot(mb_ref[rows, :], w_ref[half:, :], preferred_element_type=F32)
        out = _residual_ln(x_ref[rows, :], y, g_ref[...], b_ref[...])
        o_ref[rows, :] = out
        ob_ref[rows, :] = out.astype(ob_ref.dtype)


def _out_ln0(mix_a, mix_b, w_out, x2d, ln_g, ln_b):
    n_tok = x2d.shape[0]
    half = D_MODEL // 2
    row = lambda i: (i, 0)
    fixed = lambda i: (0, 0)
    return pl.pallas_call(
        _out_ln0_kernel,
        grid=(n_tok // OUT_TM,),
        in_specs=[pl.BlockSpec((OUT_TM, half), row),
                  pl.BlockSpec((OUT_TM, half), row),
                  pl.BlockSpec((D_MODEL, D_MODEL), fixed),
                  pl.BlockSpec((OUT_TM, D_MODEL), row),
                  pl.BlockSpec((1, D_MODEL), fixed),
                  pl.BlockSpec((1, D_MODEL), fixed)],
        out_specs=[pl.BlockSpec((OUT_TM, D_MODEL), row), pl.BlockSpec((OUT_TM, D_MODEL), row)],
        out_shape=[jax.ShapeDtypeStruct((n_tok, D_MODEL), F32), jax.ShapeDtypeStruct((n_tok, D_MODEL), BF16)],
        compiler_params=_cparams(("parallel",)),
        name="out_ln0",
    )(mix_a, mix_b, w_out, x2d, ln_g, ln_b)


OUT_CLASSES = DIL_FAN * DIL_FAN
OUT_PER_CLASS = OUT_TM // OUT_CLASSES


def _out_ln1_kernel(m_ref, w_ref, x_ref, g_ref, b_ref, o_ref, ms_ref, ys_ref):
    n_slab = D_MODEL // LANES
    piece = OUT_TM // OUT_ROW_PIECES
    per = piece // OUT_CLASSES
    for r in range(OUT_ROW_PIECES):
        rows = slice(r * piece, (r + 1) * piece)
        for idx in range(OUT_CLASSES):
            ms_ref[r * piece + idx * per:r * piece + (idx + 1) * per, :] = m_ref[0, idx, r * per:(r + 1) * per, :]
        y = jnp.dot(ms_ref[rows, :], w_ref[...], preferred_element_type=F32)
        for idx in range(OUT_CLASSES):
            rho, kap = divmod(idx, DIL_FAN)
            for c in range(n_slab):
                ys_ref[c, pl.ds(r * piece + DIL_FAN * kap + rho, per, stride=OUT_CLASSES), :] = (
                    y[idx * per:(idx + 1) * per, c * LANES:(c + 1) * LANES])
        y_tok = jnp.concatenate([ys_ref[c, rows, :] for c in range(n_slab)], axis=-1)
        o_ref[rows, :] = _residual_ln(x_ref[rows, :], y_tok, g_ref[...], b_ref[...])


def _out_ln1(mix16, w_out, x2d, ln_g, ln_b, bsz, t_len):
    n_tok = x2d.shape[0]
    tpb = t_len // OUT_TM
    fixed = lambda i: (0, 0)
    return pl.pallas_call(
        _out_ln1_kernel,
        grid=(n_tok // OUT_TM,),
        in_specs=[pl.BlockSpec((1, OUT_CLASSES, OUT_PER_CLASS, D_MODEL), lambda i: (i // tpb, 0, i % tpb, 0)),
                  pl.BlockSpec((D_MODEL, D_MODEL), fixed),
                  pl.BlockSpec((OUT_TM, D_MODEL), lambda i: (i, 0)),
                  pl.BlockSpec((1, D_MODEL), fixed),
                  pl.BlockSpec((1, D_MODEL), fixed)],
        out_specs=pl.BlockSpec((OUT_TM, D_MODEL), lambda i: (i, 0)),
        out_shape=jax.ShapeDtypeStruct((n_tok, D_MODEL), F32),
        scratch_shapes=[pltpu.VMEM((OUT_TM, D_MODEL), BF16), pltpu.VMEM((D_MODEL // LANES, OUT_TM, LANES), F32)],
        compiler_params=_cparams(("parallel",)),
        name="out_ln1",
    )(mix16, w_out, x2d, ln_g, ln_b)


def _s5_tables(a_re, a_im, log_dt, b_re, b_im, c_re, c_im, d_skip):
    hi = lax.Precision.HIGHEST
    ar = a_re.astype(F32)
    ai = a_im.astype(F32)
    dt = jnp.exp(log_dt.astype(F32))[..., None]

    def lam_pow(k):
        kk = k.astype(F32)[:, None, None, None]
        mag = jnp.exp(kk * (ar * dt)[None])
        ang = kk * (ai * dt)[None]
        return mag * jnp.cos(ang), mag * jnp.sin(ang)

    pw_re, pw_im = lam_pow(jnp.arange(S5_CHUNK + 1))
    lb_re, lb_im = pw_re[1], pw_im[1]
    nr = lb_re - 1.0
    den = jnp.square(ar) + jnp.square(ai)
    f_re = (nr * ar + lb_im * ai) / den
    f_im = (lb_im * ar - nr * ai) / den
    br = b_re.astype(F32)[None]
    bi = b_im.astype(F32)[None]
    bb_re = f_re[..., None] * br - f_im[..., None] * bi
    bb_im = f_re[..., None] * bi + f_im[..., None] * br
    cr = c_re.astype(F32)
    ci = c_im.astype(F32)

    L = S5_CHUNK


    def w_in(direction, powers):
        p_re = pw_re[powers, direction]
        p_im = pw_im[powers, direction]
        wr = jnp.einsum('sgp,gpc->gscp', p_re, bb_re[direction]) - jnp.einsum('sgp,gpc->gscp', p_im, bb_im[direction])
        wi = jnp.einsum('sgp,gpc->gscp', p_re, bb_im[direction]) + jnp.einsum('sgp,gpc->gscp', p_im, bb_re[direction])
        return jnp.stack([wr, wi], axis=3)

    s_idx = jnp.arange(L)
    c_in = jnp.stack([w_in(0, L - 1 - s_idx), w_in(1, s_idx)], axis=3)
    c_in = c_in.reshape(S5_OCTETS, 8, L, SSM_GROUP, 2, 2, SSM_STATE).transpose(0, 2, 1, 3, 4, 5, 6)
    c_in = c_in.reshape(S5_OCTETS, S5_FEAT, S5_COMPACT)

    def w_out(direction, powers):
        p_re = pw_re[powers, direction]
        p_im = pw_im[powers, direction]
        wr = jnp.einsum('gop,tgp->gpto', cr[direction], p_re) - jnp.einsum('gop,tgp->gpto', ci[direction], p_im)
        wi = -(jnp.einsum('gop,tgp->gpto', cr[direction], p_im) + jnp.einsum('gop,tgp->gpto', ci[direction], p_re))
        return jnp.stack([wr, wi], axis=1)

    c_out = jnp.stack([w_out(0, s_idx + 1), w_out(1, L - s_idx)], axis=0)
    c_out = c_out.reshape(2, S5_OCTETS, 8, 2, SSM_STATE, L, SSM_GROUP).transpose(1, 0, 3, 2, 4, 5, 6)
    c_out = c_out.reshape(S5_OCTETS, S5_FEAT, S5_COMPACT)

    def k_lag(direction):
        p_re = pw_re[:L, direction]
        p_im = pw_im[:L, direction]
        cl_re = jnp.einsum('gop,kgp->kgop', cr[direction], p_re) - jnp.einsum('gop,kgp->kgop', ci[direction], p_im)
        cl_im = jnp.einsum('gop,kgp->kgop', cr[direction], p_im) + jnp.einsum('gop,kgp->kgop', ci[direction], p_re)
        return (jnp.einsum('kgop,gpi->gkio', cl_re, bb_re[direction], precision=hi)
                - jnp.einsum('kgop,gpi->gkio', cl_im, bb_im[direction], precision=hi))

    kf = k_lag(0)
    kr = k_lag(1)
    lag = s_idx[None, :] - s_idx[:, None]
    fwd = jnp.where((lag >= 0)[None, :, :, None, None], kf[:, jnp.clip(lag, 0, L - 1)], 0.0)
    rev = jnp.where((lag <= 0)[None, :, :, None, None], kr[:, jnp.clip(-lag, 0, L - 1)], 0.0)
    skip = (jnp.eye(L, dtype=F32)[None, :, :, None, None]
            * (jnp.eye(SSM_GROUP, dtype=F32)[None] * d_skip.astype(F32).reshape(SSM_GROUPS, 1, SSM_GROUP))[:, None, None])
    c_x = (fwd + rev + skip).reshape(S5_OCTETS, 8, L, L, SSM_GROUP, SSM_GROUP).transpose(0, 2, 1, 4, 3, 5)
    c_x = c_x.reshape(S5_OCTETS, S5_FEAT, S5_COMPACT)

    q_re, q_im = lam_pow(L * jnp.arange(1, 9))
    rows = jnp.arange(8)

    def lanes(x):
        return x.reshape(x.shape[:-2] + (S5_OCTETS, S5_HALF))

    def hs_tab(direction):
        out = []
        for part in (q_re, q_im):
            for sh in (1, 2, 4):
                coef = lanes(part[sh - 1, direction])
                mask = (rows >= sh) if direction == 0 else (rows <= 7 - sh)
                out.append(jnp.where(mask[None, :, None], coef[:, None, :], 0.0))
        for part in (q_re, q_im):
            idx = rows if direction == 0 else 7 - rows
            out.append(lanes(part[idx, direction]).transpose(1, 0, 2))
        return out

    tab = jnp.stack(hs_tab(0) + hs_tab(1), axis=1)
    return c_in.astype(BF16), c_x.astype(BF16), c_out.astype(BF16), tab


S5_EXPAND_ROWS = 256


def _s5_expand_kernel(c_ref, e_ref, o_ref, *, row_span, col_span):
    for r in range(S5_FEAT // S5_EXPAND_ROWS):
        rows = slice(r * S5_EXPAND_ROWS, (r + 1) * S5_EXPAND_ROWS)
        w = jnp.dot(c_ref[0, rows, :], e_ref[...], preferred_element_type=F32)
        ri = lax.broadcasted_iota(jnp.int32, w.shape, 0) + r * S5_EXPAND_ROWS
        ci = lax.broadcasted_iota(jnp.int32, w.shape, 1)
        same = ((ri % (8 * row_span)) // row_span) == ((ci % (8 * col_span)) // col_span)
        o_ref[0, rows, :] = jnp.where(same, w, 0.0).astype(o_ref.dtype)


def _s5_expand(compact, col_span, row_span):
    k = jnp.arange(S5_COMPACT)[:, None]
    c = jnp.arange(S5_FEAT)[None, :]
    spread = ((k // col_span == c // (8 * col_span)) & (k % col_span == c % col_span)).astype(BF16)
    return pl.pallas_call(
        functools.partial(_s5_expand_kernel, row_span=row_span, col_span=col_span),
        grid=(S5_OCTETS,),
        in_specs=[pl.BlockSpec((1, S5_FEAT, S5_COMPACT), lambda o: (o, 0, 0)),
                  pl.BlockSpec((S5_COMPACT, S5_FEAT), lambda o: (0, 0))],
        out_specs=pl.BlockSpec((1, S5_FEAT, S5_FEAT), lambda o: (o, 0, 0)),
        out_shape=jax.ShapeDtypeStruct((S5_OCTETS, S5_FEAT, S5_FEAT), BF16),
        compiler_params=_cparams(("parallel",)),
        name="s5_expand",
    )(compact, spread)


def _s5_kernel(u_ref, wi_ref, wx_ref, wo_ref, tab_ref, y_ref, s_ref, *, seqs, nchunk):
    u = u_ref[0]
    s_ref[...] = jnp.dot(u, wi_ref[0], preferred_element_type=F32)
    nv = nchunk // SUBLANES
    row = lax.broadcasted_iota(jnp.int32, (SUBLANES, S5_HALF), 0)
    zero = jnp.zeros((SUBLANES, S5_HALF), F32)
    H = S5_HALF

    def bcast(x, r):
        return jnp.broadcast_to(x[r:r + 1, :], x.shape)

    def scan_block(raw_re, raw_im, nb_raw_re, nb_raw_im, nb_e_re, nb_e_im, tab0, forward):
        edge, src = (0, SUBLANES - 1) if forward else (SUBLANES - 1, 0)
        step = 1 if forward else SUBLANES - 1
        x_re = jnp.where(row == edge, bcast(nb_raw_re, src), pltpu.roll(raw_re, step, 0))
        x_im = jnp.where(row == edge, bcast(nb_raw_im, src), pltpu.roll(raw_im, step, 0))
        for k, sh in enumerate((1, 2, 4)):
            c_re = tab_ref[0, tab0 + k]
            c_im = tab_ref[0, tab0 + 3 + k]
            amt = sh if forward else SUBLANES - sh
            s_re = pltpu.roll(x_re, amt, 0)
            s_im = pltpu.roll(x_im, amt, 0)
            x_re, x_im = x_re + c_re * s_re - c_im * s_im, x_im + c_re * s_im + c_im * s_re
        p_re = tab_ref[0, tab0 + 6]
        p_im = tab_ref[0, tab0 + 7]
        k_re = bcast(nb_e_re, src)
        k_im = bcast(nb_e_im, src)
        return x_re + p_re * k_re - p_im * k_im, x_im + p_re * k_im + p_im * k_re

    y_x = jnp.dot(u, wx_ref[0], preferred_element_type=F32)

    for b in range(seqs):
        base = b * nchunk
        f_raw_re = f_raw_im = f_e_re = f_e_im = r_raw_re = r_raw_im = r_e_re = r_e_im = zero
        for v in range(nv):
            fr = slice(base + v * SUBLANES, base + (v + 1) * SUBLANES)
            raw_re = s_ref[fr, 0:H]
            raw_im = s_ref[fr, H:2 * H]
            f_e_re, f_e_im = scan_block(raw_re, raw_im, f_raw_re, f_raw_im, f_e_re, f_e_im, 0, True)
            f_raw_re, f_raw_im = raw_re, raw_im
            s_ref[fr, 0:H] = f_e_re
            s_ref[fr, H:2 * H] = f_e_im
            rr = slice(base + (nv - 1 - v) * SUBLANES, base + (nv - v) * SUBLANES)
            rraw_re = s_ref[rr, 2 * H:3 * H]
            rraw_im = s_ref[rr, 3 * H:4 * H]
            r_e_re, r_e_im = scan_block(rraw_re, rraw_im, r_raw_re, r_raw_im, r_e_re, r_e_im, 8, False)
            r_raw_re, r_raw_im = rraw_re, rraw_im
            s_ref[rr, 2 * H:3 * H] = r_e_re
            s_ref[rr, 3 * H:4 * H] = r_e_im

    y = y_x + jnp.dot(s_ref[...].astype(BF16), wo_ref[0], preferred_element_type=F32)
    y_ref[0] = y.astype(y_ref.dtype)


def _s5_scan(u, w_in, w_x, w_out, tab, t_len):
    rows = u.shape[1]
    nchunk = t_len // S5_CHUNK
    seqs = S5_ROWS // nchunk
    wspec = pl.BlockSpec((1, S5_FEAT, S5_FEAT), lambda o, i: (o, 0, 0), pipeline_mode=pl.Buffered(1))
    return pl.pallas_call(
        functools.partial(_s5_kernel, seqs=seqs, nchunk=nchunk),
        grid=(S5_OCTETS, rows // S5_ROWS),
        in_specs=[pl.BlockSpec((1, S5_ROWS, S5_FEAT), lambda o, i: (o, i, 0)),
                  wspec, wspec, wspec,
                  pl.BlockSpec((1, 16, SUBLANES, S5_HALF), lambda o, i: (o, 0, 0, 0))],
        out_specs=pl.BlockSpec((1, S5_ROWS, S5_FEAT), lambda o, i: (o, i, 0)),
        out_shape=jax.ShapeDtypeStruct((S5_OCTETS, rows, S5_FEAT), BF16),
        scratch_shapes=[pltpu.VMEM((S5_ROWS, 4 * S5_HALF), F32)],
        compiler_params=_cparams(("arbitrary", "arbitrary")),
        name="s5_scan",
    )(u, w_in, w_x, w_out, tab)


GLU_TM = 512


def _glu_kernel(y_ref, ag_ref, w_ref, b_ref, o_ref, ys_ref):
    rows = y_ref.shape[1]
    for g in range(S5_OCTETS):
        for s in range(S5_CHUNK):
            ys_ref[g, pl.ds(s, rows, stride=S5_CHUNK), :] = y_ref[g, :, s * LANES:(s + 1) * LANES].astype(F32)
    y = jnp.concatenate([ys_ref[g] for g in range(S5_OCTETS)], axis=-1)
    yg = jax.nn.gelu(y)
    z = jnp.dot(yg.astype(BF16), w_ref[...], preferred_element_type=F32) + b_ref[...]
    ya = yg * jax.nn.sigmoid(z)
    ag = ag_ref[...].astype(F32)
    o_ref[...] = (ya * (ag * jax.nn.sigmoid(ag))).astype(o_ref.dtype)


def _glu(y_oct, h_rest, glu_w, glu_b):
    n_tok = y_oct.shape[1] * S5_CHUNK
    return pl.pallas_call(
        _glu_kernel,
        grid=(n_tok // GLU_TM,),
        in_specs=[pl.BlockSpec((S5_OCTETS, GLU_TM // S5_CHUNK, S5_FEAT), lambda i: (0, i, 0)),
                  pl.BlockSpec((GLU_TM, SSM_WIDTH), lambda i: (i, 0)),
                  pl.BlockSpec((SSM_WIDTH, SSM_WIDTH), lambda i: (0, 0)),
                  pl.BlockSpec((1, SSM_WIDTH), lambda i: (0, 0))],
        out_specs=pl.BlockSpec((GLU_TM, SSM_WIDTH), lambda i: (i, 0)),
        out_shape=jax.ShapeDtypeStruct((n_tok, SSM_WIDTH), BF16),
        scratch_shapes=[pltpu.VMEM((S5_OCTETS, GLU_TM, LANES), F32)],
        compiler_params=_cparams(("parallel",)),
        name="s5_glu",
    )(y_oct, h_rest, glu_w, glu_b)


GQA_TQ = 256
GQA_TK = 512
GQA_Q_BLK0 = 2
GQA_BG_BLK0 = 5
GQA_K_BLK0 = 16
GQA_V_BLK0 = 18


def _gqa_kernel(q_ref, k_ref, v_ref, bg_ref, o_ref, qs_ref, kt_ref, va_ref, m_ref, acc_ref, *, t_len):
    tq = GQA_TQ
    n_chunks = t_len // GQA_TK

    @pl.when(pl.program_id(2) == 0)
    def _():
        lane = lax.broadcasted_iota(jnp.int32, (t_len, HEAD_DIM), 1)
        va_ref[:, :HEAD_DIM] = v_ref[...]
        va_ref[:, HEAD_DIM:] = jnp.where(lane == 0, 1.0, 0.0).astype(BF16)
        for c in range(n_chunks):
            cols = slice(c * GQA_TK, (c + 1) * GQA_TK)
            kt_ref[:, cols] = k_ref[cols, :].T

    for g in range(B_GROUP):
        qs_ref[g * tq:(g + 1) * tq, :] = q_ref[:, g * HEAD_DIM:(g + 1) * HEAD_DIM]
    m_ref[...] = jnp.full(m_ref.shape, -jnp.inf, F32)
    acc_ref[...] = jnp.zeros(acc_ref.shape, F32)

    for c in range(n_chunks):
        cols = slice(c * GQA_TK, (c + 1) * GQA_TK)
        s = jnp.dot(qs_ref[...], kt_ref[:, cols], preferred_element_type=F32)
        m_old = m_ref[...]
        m_new = jnp.maximum(m_old, jnp.max(s, axis=-1, keepdims=True))
        p = jnp.exp2(s - jnp.concatenate([m_new] * (GQA_TK // LANES), axis=1))
        alpha = jnp.exp2(m_old - m_new)
        acc_ref[...] = (jnp.concatenate([alpha, alpha], axis=1) * acc_ref[...]
                        + jnp.dot(p.astype(BF16), va_ref[cols, :], preferred_element_type=F32))
        m_ref[...] = m_new

    acc = acc_ref[...]
    o = acc[:, :HEAD_DIM] / acc[:, HEAD_DIM:HEAD_DIM + 1]
    for g in range(B_GROUP):
        sl = slice(g * HEAD_DIM, (g + 1) * HEAD_DIM)
        bg = bg_ref[:, sl].astype(F32)
        o_ref[:, sl] = (o[g * tq:(g + 1) * tq, :] * (bg * jax.nn.sigmoid(bg))).astype(o_ref.dtype)


def _gqa(h_rest, bsz, t_len):
    n_tok = h_rest.shape[0]
    nq = t_len // GQA_TQ
    rows = B_GROUP * GQA_TQ
    return pl.pallas_call(
        functools.partial(_gqa_kernel, t_len=t_len),
        grid=(bsz, B_KV_HEADS, nq),
        in_specs=[pl.BlockSpec((GQA_TQ, B_GROUP * HEAD_DIM), lambda b, h, i: (b * nq + i, GQA_Q_BLK0 + h)),
                  pl.BlockSpec((t_len, HEAD_DIM), lambda b, h, i: (b, GQA_K_BLK0 + h)),
                  pl.BlockSpec((t_len, HEAD_DIM), lambda b, h, i: (b, GQA_V_BLK0 + h)),
                  pl.BlockSpec((GQA_TQ, B_GROUP * HEAD_DIM), lambda b, h, i: (b * nq + i, GQA_BG_BLK0 + h))],
        out_specs=pl.BlockSpec((GQA_TQ, B_GROUP * HEAD_DIM), lambda b, h, i: (b * nq + i, h)),
        out_shape=jax.ShapeDtypeStruct((n_tok, ATT_WIDTH), BF16),
        scratch_shapes=[pltpu.VMEM((rows, HEAD_DIM), BF16),
                        pltpu.VMEM((HEAD_DIM, t_len), BF16),
                        pltpu.VMEM((t_len, 2 * HEAD_DIM), BF16),
                        pltpu.VMEM((rows, LANES), F32),
                        pltpu.VMEM((rows, 2 * HEAD_DIM), F32)],
        compiler_params=_cparams(("parallel", "parallel", "arbitrary")),
        name="gqa_attn",
    )(h_rest, h_rest, h_rest, h_rest)


DIL_BLOCK = 128
DIL_WIN = DIL_BLOCK + 2 * C_HALF_SPAN
DIL_TQ_MAX = 512


def _dil_kernel(*refs, tq, n_sub, has_prev, is_last):
    it = iter(refs)
    q_ref = next(it)
    km_ref, kb_ref, ka_ref = next(it), next(it), next(it)
    vm_ref, vb_ref, va_ref = next(it), next(it), next(it)
    po_ref = next(it) if has_prev else None
    pl_ref = next(it) if has_prev else None
    g_ref = next(it) if is_last else None
    o_ref = next(it)
    lse_ref = None if is_last else next(it)
    acc_scr = next(it)
    oscr = None if is_last else next(it)
    lscr = None if is_last else next(it)

    j = pl.program_id(2)
    hs = C_HALF_SPAN
    qi = lax.broadcasted_iota(jnp.int32, (DIL_BLOCK, DIL_WIN), 0)
    ci = lax.broadcasted_iota(jnp.int32, (DIL_BLOCK, DIL_WIN), 1)
    rel = ci - qi
    band = (rel >= 0) & (rel <= 2 * hs)
    lane = lax.broadcasted_iota(jnp.int32, (DIL_BLOCK, LANES), 1)
    ones = jnp.ones((DIL_WIN, HEAD_DIM), BF16)

    def window(main_ref, before_ref, after_ref, r0, sl):
        lo, hi = r0 - hs, r0 + DIL_BLOCK + hs
        parts = []
        if lo < 0:
            parts.append(before_ref[0, 0, hs + lo:hs, sl])
        parts.append(main_ref[0, 0, max(lo, 0):min(hi, tq), sl])
        if hi > tq:
            parts.append(after_ref[0, 0, 0:hi - tq, sl])
        return parts[0] if len(parts) == 1 else jnp.concatenate(parts, axis=0)

    for i in range(tq // DIL_BLOCK):
        r0 = i * DIL_BLOCK
        rows = slice(r0, r0 + DIL_BLOCK)
        pos = ci + (j * tq + r0 - hs)
        mask = band & (pos >= 0) & (pos < n_sub)
        m_tile = jnp.zeros((DIL_BLOCK, LANES), F32)
        l_tile = jnp.ones((DIL_BLOCK, LANES), F32)
        for h in range(C_HEADS):
            sl = slice(h * HEAD_DIM, (h + 1) * HEAD_DIM)
            qb = q_ref[0, 0, rows, sl]
            kw = window(km_ref, kb_ref, ka_ref, r0, sl)
            vw = jnp.concatenate([window(vm_ref, vb_ref, va_ref, r0, sl), ones], axis=1)
            s = lax.dot_general(qb, kw, (((1,), (1,)), ((), ())), preferred_element_type=F32)
            s = jnp.where(mask, s, -jnp.inf)
            m = jnp.max(s, axis=-1, keepdims=True)
            p = jnp.exp2(s - m)
            pv = jnp.dot(p.astype(BF16), vw, preferred_element_type=F32)
            acc_scr[rows, sl] = pv[:, :HEAD_DIM]
            m_tile = jnp.where(lane == h, m, m_tile)
            l_tile = jnp.where(lane == h, pv[:, HEAD_DIM:], l_tile)
        lse_t = m_tile + jnp.log2(l_tile)
        if has_prev:
            lp_t = pl_ref[0, 0, rows, :]
            mx = jnp.maximum(lp_t, lse_t)
            e1 = jnp.exp2(lp_t - mx)
            e2 = jnp.exp2(lse_t - mx)
            den = e1 + e2
            w_prev = e1 / den
            w_cur = e2 / (den * l_tile)
            lse_t = mx + jnp.log2(den)
        else:
            w_prev = None
            w_cur = 1.0 / l_tile
        for h in range(C_HEADS):
            sl = slice(h * HEAD_DIM, (h + 1) * HEAD_DIM)
            o = acc_scr[rows, sl] * w_cur[:, h:h + 1]
            if has_prev:
                o = o + po_ref[0, 0, rows, sl].astype(F32) * w_prev[:, h:h + 1]
            if is_last:
                g = g_ref[0, 0, rows, sl].astype(F32)
                o_ref[0, 0, rows, sl] = (o * (g * jax.nn.sigmoid(g))).astype(o_ref.dtype)
            else:
                oscr[h, rows, :] = o
        if not is_last:
            lscr[rows, :] = lse_t

    if not is_last:
        sub = tq // DIL_FAN
        for kap in range(DIL_FAN):
            for h in range(C_HEADS):
                o_ref[0, 0, kap, :, h * HEAD_DIM:(h + 1) * HEAD_DIM] = (
                    oscr[h, pl.ds(kap, sub, stride=DIL_FAN), :].astype(o_ref.dtype))
            lse_ref[0, 0, kap] = lscr[pl.ds(kap, sub, stride=DIL_FAN), :]


def _dil_pass(src, bsz, t_len, d, prev):
    n_sub = t_len // d
    tq = min(DIL_TQ_MAX, n_sub)
    nt = n_sub // tq
    hs = C_HALF_SPAN
    hb = tq // hs
    last_halo = n_sub // hs - 1
    width = C_HEADS * HEAD_DIM
    f = DIL_FAN
    is_last = d == C_DILATIONS[-1]
    has_prev = prev is not None

    def main(sec, w=width):
        return pl.BlockSpec((1, 1, tq, w), lambda b, r, j: (b, r, j, sec))

    def before(sec):
        return pl.BlockSpec((1, 1, hs, width), lambda b, r, j: (b, r, jnp.maximum(j * hb - 1, 0), sec))

    def after(sec):
        return pl.BlockSpec((1, 1, hs, width), lambda b, r, j: (b, r, jnp.minimum((j + 1) * hb, last_halo), sec))

    in_specs = [main(0), main(1), before(1), after(1), main(2), before(2), after(2)]
    args = [src] * 7
    if has_prev:
        in_specs += [main(0), main(0, LANES)]
        args += list(prev)
    if is_last:
        in_specs.append(main(3))
        args.append(src)
    scratch = [pltpu.VMEM((tq, width), F32)]
    if is_last:
        out_specs = main(0)
        out_shape = jax.ShapeDtypeStruct((bsz, d, n_sub, width), BF16)
    else:
        out_specs = [pl.BlockSpec((1, 1, f, tq // f, width), lambda b, r, j: (b, r, 0, j, 0)),
                     pl.BlockSpec((1, 1, f, tq // f, LANES), lambda b, r, j: (b, r, 0, j, 0))]
        out_shape = [jax.ShapeDtypeStruct((bsz, d, f, n_sub // f, width), BF16),
                     jax.ShapeDtypeStruct((bsz, d, f, n_sub // f, LANES), F32)]
        scratch += [pltpu.VMEM((C_HEADS, tq, HEAD_DIM), F32), pltpu.VMEM((tq, LANES), F32)]
    res = pl.pallas_call(
        functools.partial(_dil_kernel, tq=tq, n_sub=n_sub, has_prev=has_prev, is_last=is_last),
        grid=(bsz, d, nt),
        in_specs=in_specs,
        out_specs=out_specs,
        out_shape=out_shape,
        scratch_shapes=scratch,
        compiler_params=_cparams(("parallel", "parallel", "arbitrary")),
        name=f"dilated_d{d}",
    )(*args)
    if is_last:
        return res
    return (res[0].reshape(bsz, d * f, n_sub // f, width), res[1].reshape(bsz, d * f, n_sub // f, LANES))


def _trunk(x, params):
    bsz, t_len, _ = x.shape
    x2d = x.reshape(bsz * t_len, D_MODEL)
    p = params
    au = _proj_au(x2d, p["w_au"])
    h_rest = _proj0(x2d, p["w_rest"], p["ax_cos"][t_len], p["ax_sin"][t_len], p["qn_g"], p["kn_g"], t_len)
    y_oct = _s5_scan(au, p["s5_w_in"], p["s5_w_x"], p["s5_w_out"], p["s5_tab"], t_len)
    mix_a = _glu(y_oct, h_rest, p["glu_w"], p["glu_b"])
    mix_b = _gqa(h_rest, bsz, t_len)
    x1, x1b = _out_ln0(mix_a, mix_b, p["even_w_out"], x2d, p["ln_g0"], p["ln_b0"])
    h1, h4, h16 = _proj1(x1b, p["odd_w_in"], p["pr_cos"][t_len], p["pr_sin"][t_len], bsz, t_len)
    prev = _dil_pass(h1.reshape(bsz, 1, t_len, h1.shape[1]), bsz, t_len, 1, None)
    prev = _dil_pass(h4, bsz, t_len, DIL_FAN, prev)
    n16 = t_len // OUT_CLASSES
    mix16 = _dil_pass(h16.reshape(bsz, OUT_CLASSES, n16, h16.shape[-1]), bsz, t_len, OUT_CLASSES, prev)
    x2 = _out_ln1(mix16, p["odd_w_out"], x1, p["ln_g1"], p["ln_b1"], bsz, t_len)
    return x2.reshape(bsz, t_len, D_MODEL)


def kernel(x_prompt, x_sample, even_w_in, even_w_out, ssm_a_re, ssm_a_im, ssm_log_dt, ssm_b_re, ssm_b_im, ssm_c_re, ssm_c_im, ssm_d, ssm_glu_w, ssm_glu_b, attn_q_norm, attn_k_norm, odd_w_in, odd_w_out, ln_g, ln_b):
    c_in, c_x, c_out, tab = _s5_tables(ssm_a_re[0], ssm_a_im[0], ssm_log_dt[0], ssm_b_re[0], ssm_b_im[0],
                                       ssm_c_re[0], ssm_c_im[0], ssm_d[0])
    w_in = _s5_expand(c_in, SSM_STATE, SSM_GROUP)
    w_x = _s5_expand(c_x, SSM_GROUP, SSM_GROUP)
    w_out = _s5_expand(c_out, SSM_GROUP, SSM_STATE)
    lens = sorted({x_prompt.shape[1], x_sample.shape[1]})
    ax = {t: _axial_tables(t) for t in lens}
    pr = {t: _partial_tables(t) for t in lens}
    params = {
        "w_au": even_w_in[0][:, :SSM_WIDTH].astype(BF16),
        "w_rest": even_w_in[0][:, SSM_WIDTH:].astype(BF16),
        "even_w_out": even_w_out[0].astype(BF16),
        "odd_w_in": odd_w_in[0].astype(BF16),
        "odd_w_out": odd_w_out[0].astype(BF16),
        "glu_w": ssm_glu_w[0].astype(BF16),
        "glu_b": ssm_glu_b[0].astype(F32).reshape(1, SSM_WIDTH),
        "qn_g": attn_q_norm[0].astype(F32).reshape(1, HEAD_DIM),
        "kn_g": attn_k_norm[0].astype(F32).reshape(1, HEAD_DIM),
        "ln_g0": ln_g[0].astype(F32).reshape(1, D_MODEL),
        "ln_b0": ln_b[0].astype(F32).reshape(1, D_MODEL),
        "ln_g1": ln_g[1].astype(F32).reshape(1, D_MODEL),
        "ln_b1": ln_b[1].astype(F32).reshape(1, D_MODEL),
        "s5_w_in": w_in, "s5_w_x": w_x, "s5_w_out": w_out, "s5_tab": tab,
        "ax_cos": {t: ax[t][0] for t in lens}, "ax_sin": {t: ax[t][1] for t in lens},
        "pr_cos": {t: pr[t][0] for t in lens}, "pr_sin": {t: pr[t][1] for t in lens},
    }
    return (_trunk(x_prompt, params), _trunk(x_sample, params))
```

```python
import functools
import math

import jax
import jax.numpy as jnp
from jax import lax
from jax.experimental import pallas as pl
from jax.experimental.pallas import tpu as pltpu

F32 = jnp.float32
BF16 = jnp.bfloat16

D_MODEL = 2048
HEAD_DIM = 128
GRID_W = 64
SSM_WIDTH = 1024
SSM_GROUP = 16
SSM_GROUPS = 64
SSM_STATE = 64
ATT_WIDTH = 1024
B_KV_HEADS = 2
B_GROUP = 4
KV_WIDTH = B_KV_HEADS * HEAD_DIM
AXIAL_THETA = 10000.0
C_HEADS = 16
C_DILATIONS = (1, 4, 16)
C_HALF_SPAN = 64
ROPE_THETA = 500000.0
ROPE_DIMS = 32
DEPTH = 2
DN_ALPHA = (2 * DEPTH) ** 0.25
LN_EPS = 1e-5
QK_EPS = 1e-6
ATT_SCALE = HEAD_DIM ** -0.5
LOG2E = math.log2(math.e)

LANES = 128
SUBLANES = 8
VMEM_LIMIT_BYTES = 56 * 1024 * 1024

S5_CHUNK = 16
S5_OCTETS = SSM_GROUPS // 8
S5_FEAT = S5_CHUNK * LANES
S5_HALF = 8 * SSM_STATE
S5_COMPACT = S5_FEAT // 8
S5_ROWS = 512


def _cparams(sem):
    return pltpu.CompilerParams(dimension_semantics=sem, vmem_limit_bytes=VMEM_LIMIT_BYTES)


def _axial_tables(t_len):
    pos = jnp.arange(t_len, dtype=jnp.int32)
    row = (pos // GRID_W).astype(F32)
    col = (pos % GRID_W).astype(F32)
    n = HEAD_DIM // 2
    inv = AXIAL_THETA ** (-jnp.arange(0, n, 2, dtype=F32) / n)
    ar = row[:, None] * inv[None, :]
    ac = col[:, None] * inv[None, :]
    cos = jnp.concatenate([jnp.cos(ar), jnp.cos(ar), jnp.cos(ac), jnp.cos(ac)], axis=-1)
    sin = jnp.concatenate([-jnp.sin(ar), jnp.sin(ar), -jnp.sin(ac), jnp.sin(ac)], axis=-1)
    return cos, sin


def _partial_tables(t_len):
    pos = jnp.arange(t_len, dtype=F32)
    n = ROPE_DIMS
    inv = ROPE_THETA ** (-jnp.arange(0, n, 2, dtype=F32) / n)
    ang = pos[:, None] * inv[None, :]
    rest = HEAD_DIM - n
    cos = jnp.concatenate([jnp.cos(ang), jnp.cos(ang), jnp.ones((t_len, rest), F32)], axis=-1)
    sin = jnp.concatenate([-jnp.sin(ang), jnp.sin(ang), jnp.zeros((t_len, rest), F32)], axis=-1)
    return cos, sin


def _rotate_pairs(x, cos, sin, half, period):
    lane = lax.broadcasted_iota(jnp.int32, x.shape, x.ndim - 1)
    up = pltpu.roll(x, LANES - half, x.ndim - 1)
    dn = pltpu.roll(x, half, x.ndim - 1)
    partner = jnp.where((lane % period) < half, up, dn)
    return x * cos + partner * sin


PROJ_TM = 1024


def _proj_au_kernel(x_ref, w_ref, o_ref, scr_ref):
    res = jnp.dot(x_ref[...].astype(BF16), w_ref[...], preferred_element_type=F32)
    rows = o_ref.shape[1]
    for g in range(S5_OCTETS):
        scr_ref[...] = res[:, g * LANES:(g + 1) * LANES]
        for s in range(S5_CHUNK):
            o_ref[g, :, s * LANES:(s + 1) * LANES] = scr_ref[pl.ds(s, rows, stride=S5_CHUNK), :].astype(o_ref.dtype)


PROJ_AU_TM = 512


def _proj_au(x2d, w_au):
    n_tok = x2d.shape[0]
    tm = PROJ_AU_TM
    return pl.pallas_call(
        _proj_au_kernel,
        grid=(n_tok // tm,),
        in_specs=[pl.BlockSpec((tm, D_MODEL), lambda i: (i, 0)),
                  pl.BlockSpec((D_MODEL, SSM_WIDTH), lambda i: (0, 0))],
        out_specs=pl.BlockSpec((S5_OCTETS, tm // S5_CHUNK, S5_FEAT), lambda i: (0, i, 0)),
        out_shape=jax.ShapeDtypeStruct((S5_OCTETS, n_tok // S5_CHUNK, S5_FEAT), BF16),
        scratch_shapes=[pltpu.VMEM((tm, LANES), F32)],
        compiler_params=_cparams(("parallel",)),
        name="proj_au",
    )(x2d, w_au)


P0_TN = 512
P0_Q_BLOCKS = (2, 3)
P0_KV_BLOCK = 4
P0_ROW_PIECES = 4


def _proj0_kernel(x_ref, w_ref, cos_ref, sin_ref, qg_ref, kg_ref, o_ref, xb_ref):
    j = pl.program_id(1)

    @pl.when(j == 0)
    def _():
        xb_ref[...] = x_ref[...].astype(BF16)

    is_q = (j == P0_Q_BLOCKS[0]) | (j == P0_Q_BLOCKS[1])
    is_kv = j == P0_KV_BLOCK
    gain = jnp.where(is_q, qg_ref[...], kg_ref[...])
    scale = jnp.where(is_q, ATT_SCALE * LOG2E, 1.0).astype(F32)
    tm = xb_ref.shape[0]
    piece = tm // P0_ROW_PIECES
    for r in range(P0_ROW_PIECES):
        rows = slice(r * piece, (r + 1) * piece)
        res = jnp.dot(xb_ref[rows, :], w_ref[...], preferred_element_type=F32)
        for h in range(P0_TN // HEAD_DIM):
            sl = slice(h * HEAD_DIM, (h + 1) * HEAD_DIM)
            normed = (is_q | is_kv) if h < B_KV_HEADS else is_q
            x = res[:, sl]
            ms = jnp.mean(x * x, axis=-1, keepdims=True)
            xn = x * jnp.where(normed, lax.rsqrt(ms + QK_EPS), 1.0) * jnp.where(normed, gain, 1.0)
            cos = jnp.where(normed, cos_ref[rows, :], 1.0)
            sin = jnp.where(normed, sin_ref[rows, :], 0.0)
            val = _rotate_pairs(xn, cos, sin, HEAD_DIM // 4, HEAD_DIM // 2) * scale
            o_ref[rows, sl] = val.astype(o_ref.dtype)


def _proj0(x2d, w_rest, cos, sin, qn_g, kn_g, t_len):
    n_tok = x2d.shape[0]
    n_cols = w_rest.shape[1]
    tm = min(PROJ_TM, t_len)
    tpb = t_len // tm
    return pl.pallas_call(
        _proj0_kernel,
        grid=(n_tok // tm, n_cols // P0_TN),
        in_specs=[pl.BlockSpec((tm, D_MODEL), lambda i, j: (i, 0)),
                  pl.BlockSpec((D_MODEL, P0_TN), lambda i, j: (0, j)),
                  pl.BlockSpec((tm, HEAD_DIM), lambda i, j: (i % tpb, 0)),
                  pl.BlockSpec((tm, HEAD_DIM), lambda i, j: (i % tpb, 0)),
                  pl.BlockSpec((1, HEAD_DIM), lambda i, j: (0, 0)),
                  pl.BlockSpec((1, HEAD_DIM), lambda i, j: (0, 0))],
        out_specs=pl.BlockSpec((tm, P0_TN), lambda i, j: (i, j)),
        out_shape=jax.ShapeDtypeStruct((n_tok, n_cols), BF16),
        scratch_shapes=[pltpu.VMEM((tm, D_MODEL), BF16)],
        compiler_params=_cparams(("parallel", "arbitrary")),
        name="proj0",
    )(x2d, w_rest, cos, sin, qn_g, kn_g)


P1_TM = 2048
P1_TN = 512
P1_ROT_BLOCKS = 2 * (C_HEADS * HEAD_DIM) // P1_TN
P1_Q_BLOCKS = (C_HEADS * HEAD_DIM) // P1_TN
P1_ROW_PIECES = 4
DIL_FAN = 4


def _proj1_kernel(x_ref, w_ref, cos_ref, sin_ref, o_ref, o4_ref, o16_ref, rs_ref, r4_ref):
    j = pl.program_id(1)
    heads = P1_TN // HEAD_DIM
    slabs = [slice(h * HEAD_DIM, (h + 1) * HEAD_DIM) for h in range(heads)]

    rotated = j < P1_ROT_BLOCKS
    scale = jnp.where(j < P1_Q_BLOCKS, ATT_SCALE * LOG2E, 1.0).astype(F32)
    piece = rs_ref.shape[1] // P1_ROW_PIECES
    r4 = piece // DIL_FAN
    r16 = piece // (DIL_FAN * DIL_FAN)
    for r in range(P1_ROW_PIECES):
        rows = slice(r * piece, (r + 1) * piece)
        res = jnp.dot(x_ref[rows, :], w_ref[...], preferred_element_type=F32)
        cos = jnp.where(rotated, cos_ref[rows, :], 1.0)
        sin = jnp.where(rotated, sin_ref[rows, :], 0.0)
        for h, sl in enumerate(slabs):
            val = _rotate_pairs(res[:, sl], cos, sin, ROPE_DIMS // 2, HEAD_DIM) * scale
            o_ref[rows, sl] = val.astype(o_ref.dtype)
            rs_ref[h, rows, :] = val
            for rho in range(DIL_FAN):
                cls = rs_ref[h, pl.ds(r * piece + rho, r4, stride=DIL_FAN), :]
                r4_ref[h, r * piece + rho * r4:r * piece + (rho + 1) * r4, :] = cls
                o4_ref[0, rho, r * r4:(r + 1) * r4, sl] = cls.astype(o4_ref.dtype)
            for rho in range(DIL_FAN):
                for kap in range(DIL_FAN):
                    o16_ref[0, rho, kap, r * r16:(r + 1) * r16, sl] = r4_ref[
                        h, pl.ds(r * piece + rho * r4 + kap, r16, stride=DIL_FAN), :].astype(o16_ref.dtype)


def _proj1(xb2d, w, cos, sin, bsz, t_len):
    n_tok = xb2d.shape[0]
    n_cols = w.shape[1]
    tm = min(P1_TM, t_len)
    tpb = t_len // tm
    f = DIL_FAN
    return pl.pallas_call(
        _proj1_kernel,
        grid=(n_tok // tm, n_cols // P1_TN),
        in_specs=[pl.BlockSpec((tm, D_MODEL), lambda i, j: (i, 0)),
                  pl.BlockSpec((D_MODEL, P1_TN), lambda i, j: (0, j)),
                  pl.BlockSpec((tm, HEAD_DIM), lambda i, j: (i % tpb, 0)),
                  pl.BlockSpec((tm, HEAD_DIM), lambda i, j: (i % tpb, 0))],
        out_specs=[pl.BlockSpec((tm, P1_TN), lambda i, j: (i, j)),
                   pl.BlockSpec((1, f, tm // f, P1_TN), lambda i, j: (i // tpb, 0, i % tpb, j)),
                   pl.BlockSpec((1, f, f, tm // (f * f), P1_TN), lambda i, j: (i // tpb, 0, 0, i % tpb, j))],
        out_shape=[jax.ShapeDtypeStruct((n_tok, n_cols), BF16),
                   jax.ShapeDtypeStruct((bsz, f, t_len // f, n_cols), BF16),
                   jax.ShapeDtypeStruct((bsz, f, f, t_len // (f * f), n_cols), BF16)],
        scratch_shapes=[pltpu.VMEM((P1_TN // HEAD_DIM, tm, HEAD_DIM), F32),
                        pltpu.VMEM((P1_TN // HEAD_DIM, tm, HEAD_DIM), F32)],
        compiler_params=_cparams(("parallel", "arbitrary")),
        name="proj1",
    )(xb2d, w, cos, sin)


OUT_TM = 512
OUT_ROW_PIECES = 2


def _residual_ln(x, y, g, b):
    z = DN_ALPHA * x + y
    mu = jnp.mean(z, axis=-1, keepdims=True)
    zc = z - mu
    var = jnp.mean(zc * zc, axis=-1, keepdims=True)
    return zc * lax.rsqrt(var + LN_EPS) * g + b


def _out_ln0_kernel(ma_ref, mb_ref, w_ref, x_ref, g_ref, b_ref, o_ref, ob_ref):
    half = ma_ref.shape[1]
    piece = OUT_TM // OUT_ROW_PIECES
    for r in range(OUT_ROW_PIECES):
        rows = slice(r * piece, (r + 1) * piece)
        y = jnp.dot(ma_ref[rows, :], w_ref[:half, :], preferred_element_type=F32)
        y = y + jnp.dot(mb_ref[rows, :], w_ref[half:, :], preferred_element_type=F32)
        out = _residual_ln(x_ref[rows, :], y, g_ref[...], b_ref[...])
        o_ref[rows, :] = out
        ob_ref[rows, :] = out.astype(ob_ref.dtype)


def _out_ln0(mix_a, mix_b, w_out, x2d, ln_g, ln_b):
    n_tok = x2d.shape[0]
    half = D_MODEL // 2
    row = lambda i: (i, 0)
    fixed = lambda i: (0, 0)
    return pl.pallas_call(
        _out_ln0_kernel,
        grid=(n_tok // OUT_TM,),
        in_specs=[pl.BlockSpec((OUT_TM, half), row),
                  pl.BlockSpec((OUT_TM, half), row),
                  pl.BlockSpec((D_MODEL, D_MODEL), fixed),
                  pl.BlockSpec((OUT_TM, D_MODEL), row),
                  pl.BlockSpec((1, D_MODEL), fixed),
                  pl.BlockSpec((1, D_MODEL), fixed)],
        out_specs=[pl.BlockSpec((OUT_TM, D_MODEL), row), pl.BlockSpec((OUT_TM, D_MODEL), row)],
        out_shape=[jax.ShapeDtypeStruct((n_tok, D_MODEL), F32), jax.ShapeDtypeStruct((n_tok, D_MODEL), BF16)],
        compiler_params=_cparams(("parallel",)),
        name="out_ln0",
    )(mix_a, mix_b, w_out, x2d, ln_g, ln_b)


OUT_CLASSES = DIL_FAN * DIL_FAN
OUT_PER_CLASS = OUT_TM // OUT_CLASSES


def _out_ln1_kernel(m_ref, p_ref, w_ref, x_ref, g_ref, b_ref, o_ref, ms_ref):
    piece = OUT_TM // OUT_ROW_PIECES
    per = piece // OUT_CLASSES
    for r in range(OUT_ROW_PIECES):
        rows = slice(r * piece, (r + 1) * piece)
        for idx in range(OUT_CLASSES):
            ms_ref[r * piece + idx * per:r * piece + (idx + 1) * per, :] = m_ref[0, idx, r * per:(r + 1) * per, :]
        m_tok = jnp.dot(p_ref[...], ms_ref[rows, :], preferred_element_type=F32).astype(BF16)
        y = jnp.dot(m_tok, w_ref[...], preferred_element_type=F32)
        o_ref[rows, :] = _residual_ln(x_ref[rows, :], y, g_ref[...], b_ref[...])


def _out_ln1(mix16, w_out, x2d, ln_g, ln_b, bsz, t_len):
    n_tok = x2d.shape[0]
    tpb = t_len // OUT_TM
    fixed = lambda i: (0, 0)
    piece = OUT_TM // OUT_ROW_PIECES
    per = piece // OUT_CLASSES
    tok = jnp.arange(piece)
    cls = tok % OUT_CLASSES
    src = ((cls % DIL_FAN) * DIL_FAN + cls // DIL_FAN) * per + tok // OUT_CLASSES
    perm = (src[:, None] == jnp.arange(piece)[None, :]).astype(BF16)
    return pl.pallas_call(
        _out_ln1_kernel,
        grid=(n_tok // OUT_TM,),
        in_specs=[pl.BlockSpec((1, OUT_CLASSES, OUT_PER_CLASS, D_MODEL), lambda i: (i // tpb, 0, i % tpb, 0)),
                  pl.BlockSpec((piece, piece), fixed),
                  pl.BlockSpec((D_MODEL, D_MODEL), fixed),
                  pl.BlockSpec((OUT_TM, D_MODEL), lambda i: (i, 0)),
                  pl.BlockSpec((1, D_MODEL), fixed),
                  pl.BlockSpec((1, D_MODEL), fixed)],
        out_specs=pl.BlockSpec((OUT_TM, D_MODEL), lambda i: (i, 0)),
        out_shape=jax.ShapeDtypeStruct((n_tok, D_MODEL), F32),
        scratch_shapes=[pltpu.VMEM((OUT_TM, D_MODEL), BF16)],
        compiler_params=_cparams(("parallel",)),
        name="out_ln1",
    )(mix16, perm, w_out, x2d, ln_g, ln_b)


def _s5_tables(a_re, a_im, log_dt, b_re, b_im, c_re, c_im, d_skip):
    hi = lax.Precision.HIGHEST
    ar = a_re.astype(F32)
    ai = a_im.astype(F32)
    dt = jnp.exp(log_dt.astype(F32))[..., None]

    def lam_pow(k):
        kk = k.astype(F32)[:, None, None, None]
        mag = jnp.exp(kk * (ar * dt)[None])
        ang = kk * (ai * dt)[None]
        return mag * jnp.cos(ang), mag * jnp.sin(ang)

    pw_re, pw_im = lam_pow(jnp.arange(S5_CHUNK + 1))
    lb_re, lb_im = pw_re[1], pw_im[1]
    nr = lb_re - 1.0
    den = jnp.square(ar) + jnp.square(ai)
    f_re = (nr * ar + lb_im * ai) / den
    f_im = (lb_im * ar - nr * ai) / den
    br = b_re.astype(F32)[None]
    bi = b_im.astype(F32)[None]
    bb_re = f_re[..., None] * br - f_im[..., None] * bi
    bb_im = f_re[..., None] * bi + f_im[..., None] * br
    cr = c_re.astype(F32)
    ci = c_im.astype(F32)

    L = S5_CHUNK


    s_idx = jnp.arange(L)

    def by_octet(x):
        return x.reshape(2, L, S5_OCTETS, 8, SSM_STATE)

    def in_pow(p):
        return by_octet(jnp.stack([p[L - 1 - s_idx, 0], p[s_idx, 1]])).transpose(2, 1, 3, 0, 4)[:, :, :, None]

    def in_b(x):
        return x.reshape(2, S5_OCTETS, 8, SSM_STATE, SSM_GROUP).transpose(1, 2, 4, 0, 3)[:, None]

    pr, pi, xr, xi = in_pow(pw_re), in_pow(pw_im), in_b(bb_re), in_b(bb_im)
    c_in = jnp.stack([pr * xr - pi * xi, pr * xi + pi * xr], axis=5)
    c_in = c_in.reshape(S5_OCTETS, S5_FEAT, S5_COMPACT)

    def out_pow(p):
        return by_octet(jnp.stack([p[s_idx + 1, 0], p[L - s_idx, 1]])).transpose(2, 0, 3, 4, 1)[..., None]

    def out_c(x):
        return x.reshape(2, S5_OCTETS, 8, SSM_GROUP, SSM_STATE).transpose(1, 0, 2, 4, 3)[:, :, :, :, None]

    qr, qi, yr, yi = out_pow(pw_re), out_pow(pw_im), out_c(cr), out_c(ci)
    c_out = jnp.stack([yr * qr - yi * qi, -(yr * qi + yi * qr)], axis=2)
    c_out = c_out.reshape(S5_OCTETS, S5_FEAT, S5_COMPACT)

    def k_lag(direction):
        p_re = pw_re[:L, direction]
        p_im = pw_im[:L, direction]
        cl_re = jnp.einsum('gop,kgp->kgop', cr[direction], p_re) - jnp.einsum('gop,kgp->kgop', ci[direction], p_im)
        cl_im = jnp.einsum('gop,kgp->kgop', cr[direction], p_im) + jnp.einsum('gop,kgp->kgop', ci[direction], p_re)
        return (jnp.einsum('kgop,gpi->gkio', cl_re, bb_re[direction], precision=hi)
                - jnp.einsum('kgop,gpi->gkio', cl_im, bb_im[direction], precision=hi))

    kf = k_lag(0)
    kr = k_lag(1)
    lag = s_idx[None, :] - s_idx[:, None]
    fwd = jnp.where((lag >= 0)[None, :, :, None, None], kf[:, jnp.clip(lag, 0, L - 1)], 0.0)
    rev = jnp.where((lag <= 0)[None, :, :, None, None], kr[:, jnp.clip(-lag, 0, L - 1)], 0.0)
    skip = (jnp.eye(L, dtype=F32)[None, :, :, None, None]
            * (jnp.eye(SSM_GROUP, dtype=F32)[None] * d_skip.astype(F32).reshape(SSM_GROUPS, 1, SSM_GROUP))[:, None, None])
    c_x = (fwd + rev + skip).reshape(S5_OCTETS, 8, L, L, SSM_GROUP, SSM_GROUP).transpose(0, 2, 1, 4, 3, 5)
    c_x = c_x.reshape(S5_OCTETS, S5_FEAT, S5_COMPACT)

    q_re, q_im = lam_pow(L * jnp.arange(1, 9))
    rows = jnp.arange(8)

    def lanes(x):
        return x.reshape(x.shape[:-2] + (S5_OCTETS, S5_HALF))

    def hs_tab(direction):
        out = []
        for part in (q_re, q_im):
            for sh in (1, 2, 4):
                coef = lanes(part[sh - 1, direction])
                mask = (rows >= sh) if direction == 0 else (rows <= 7 - sh)
                out.append(jnp.where(mask[None, :, None], coef[:, None, :], 0.0))
        for part in (q_re, q_im):
            idx = rows if direction == 0 else 7 - rows
            out.append(lanes(part[idx, direction]).transpose(1, 0, 2))
        return out

    tab = jnp.stack(hs_tab(0) + hs_tab(1), axis=1)
    return c_in.astype(BF16), c_x.astype(BF16), c_out.astype(BF16), tab


S5_EXPAND_ROWS = 256


def _s5_expand_kernel(c_ref, e_ref, o_ref, *, row_span, col_span):
    for r in range(S5_FEAT // S5_EXPAND_ROWS):
        rows = slice(r * S5_EXPAND_ROWS, (r + 1) * S5_EXPAND_ROWS)
        w = jnp.dot(c_ref[0, rows, :], e_ref[...], preferred_element_type=F32)
        ri = lax.broadcasted_iota(jnp.int32, w.shape, 0) + r * S5_EXPAND_ROWS
        ci = lax.broadcasted_iota(jnp.int32, w.shape, 1)
        same = ((ri % (8 * row_span)) // row_span) == ((ci % (8 * col_span)) // col_span)
        o_ref[0, rows, :] = jnp.where(same, w, 0.0).astype(o_ref.dtype)


def _s5_expand(compact, col_span, row_span):
    k = jnp.arange(S5_COMPACT)[:, None]
    c = jnp.arange(S5_FEAT)[None, :]
    spread = ((k // col_span == c // (8 * col_span)) & (k % col_span == c % col_span)).astype(BF16)
    return pl.pallas_call(
        functools.partial(_s5_expand_kernel, row_span=row_span, col_span=col_span),
        grid=(S5_OCTETS,),
        in_specs=[pl.BlockSpec((1, S5_FEAT, S5_COMPACT), lambda o: (o, 0, 0)),
                  pl.BlockSpec((S5_COMPACT, S5_FEAT), lambda o: (0, 0))],
        out_specs=pl.BlockSpec((1, S5_FEAT, S5_FEAT), lambda o: (o, 0, 0)),
        out_shape=jax.ShapeDtypeStruct((S5_OCTETS, S5_FEAT, S5_FEAT), BF16),
        compiler_params=_cparams(("parallel",)),
        name="s5_expand",
    )(compact, spread)


def _s5_kernel(u_ref, wi_ref, wx_ref, wo_ref, tab_ref, y_ref, s_ref, *, seqs, nchunk):
    u = u_ref[0]
    s_ref[...] = jnp.dot(u, wi_ref[0], preferred_element_type=F32)
    nv = nchunk // SUBLANES
    row = lax.broadcasted_iota(jnp.int32, (SUBLANES, S5_HALF), 0)
    zero = jnp.zeros((SUBLANES, S5_HALF), F32)
    H = S5_HALF

    def bcast(x, r):
        return jnp.broadcast_to(x[r:r + 1, :], x.shape)

    def scan_block(raw_re, raw_im, nb_raw_re, nb_raw_im, nb_e_re, nb_e_im, tab0, forward):
        edge, src = (0, SUBLANES - 1) if forward else (SUBLANES - 1, 0)
        step = 1 if forward else SUBLANES - 1
        x_re = jnp.where(row == edge, bcast(nb_raw_re, src), pltpu.roll(raw_re, step, 0))
        x_im = jnp.where(row == edge, bcast(nb_raw_im, src), pltpu.roll(raw_im, step, 0))
        for k, sh in enumerate((1, 2, 4)):
            c_re = tab_ref[0, tab0 + k]
            c_im = tab_ref[0, tab0 + 3 + k]
            amt = sh if forward else SUBLANES - sh
            s_re = pltpu.roll(x_re, amt, 0)
            s_im = pltpu.roll(x_im, amt, 0)
            x_re, x_im = x_re + c_re * s_re - c_im * s_im, x_im + c_re * s_im + c_im * s_re
        p_re = tab_ref[0, tab0 + 6]
        p_im = tab_ref[0, tab0 + 7]
        k_re = bcast(nb_e_re, src)
        k_im = bcast(nb_e_im, src)
        return x_re + p_re * k_re - p_im * k_im, x_im + p_re * k_im + p_im * k_re

    y_x = jnp.dot(u, wx_ref[0], preferred_element_type=F32)

    for b in range(seqs):
        base = b * nchunk
        f_raw_re = f_raw_im = f_e_re = f_e_im = r_raw_re = r_raw_im = r_e_re = r_e_im = zero
        for v in range(nv):
            fr = slice(base + v * SUBLANES, base + (v + 1) * SUBLANES)
            raw_re = s_ref[fr, 0:H]
            raw_im = s_ref[fr, H:2 * H]
            f_e_re, f_e_im = scan_block(raw_re, raw_im, f_raw_re, f_raw_im, f_e_re, f_e_im, 0, True)
            f_raw_re, f_raw_im = raw_re, raw_im
            s_ref[fr, 0:H] = f_e_re
            s_ref[fr, H:2 * H] = f_e_im
            rr = slice(base + (nv - 1 - v) * SUBLANES, base + (nv - v) * SUBLANES)
            rraw_re = s_ref[rr, 2 * H:3 * H]
            rraw_im = s_ref[rr, 3 * H:4 * H]
            r_e_re, r_e_im = scan_block(rraw_re, rraw_im, r_raw_re, r_raw_im, r_e_re, r_e_im, 8, False)
            r_raw_re, r_raw_im = rraw_re, rraw_im
            s_ref[rr, 2 * H:3 * H] = r_e_re
            s_ref[rr, 3 * H:4 * H] = r_e_im

    y = y_x + jnp.dot(s_ref[...].astype(BF16), wo_ref[0], preferred_element_type=F32)
    y_ref[0] = y.astype(y_ref.dtype)


def _s5_scan(u, w_in, w_x, w_out, tab, t_len):
    rows = u.shape[1]
    nchunk = t_len // S5_CHUNK
    seqs = S5_ROWS // nchunk
    wspec = pl.BlockSpec((1, S5_FEAT, S5_FEAT), lambda o, i: (o, 0, 0), pipeline_mode=pl.Buffered(1))
    return pl.pallas_call(
        functools.partial(_s5_kernel, seqs=seqs, nchunk=nchunk),
        grid=(S5_OCTETS, rows // S5_ROWS),
        in_specs=[pl.BlockSpec((1, S5_ROWS, S5_FEAT), lambda o, i: (o, i, 0)),
                  wspec, wspec, wspec,
                  pl.BlockSpec((1, 16, SUBLANES, S5_HALF), lambda o, i: (o, 0, 0, 0))],
        out_specs=pl.BlockSpec((1, S5_ROWS, S5_FEAT), lambda o, i: (o, i, 0)),
        out_shape=jax.ShapeDtypeStruct((S5_OCTETS, rows, S5_FEAT), BF16),
        scratch_shapes=[pltpu.VMEM((S5_ROWS, 4 * S5_HALF), F32)],
        compiler_params=_cparams(("arbitrary", "arbitrary")),
        name="s5_scan",
    )(u, w_in, w_x, w_out, tab)


GLU_TM = 512


def _glu_kernel(y_ref, ag_ref, w_ref, b_ref, o_ref, ys_ref):
    rows = y_ref.shape[1]
    for g in range(S5_OCTETS):
        for s in range(S5_CHUNK):
            ys_ref[g, pl.ds(s, rows, stride=S5_CHUNK), :] = y_ref[g, :, s * LANES:(s + 1) * LANES].astype(F32)
    y = jnp.concatenate([ys_ref[g] for g in range(S5_OCTETS)], axis=-1)
    yg = jax.nn.gelu(y)
    z = jnp.dot(yg.astype(BF16), w_ref[...], preferred_element_type=F32) + b_ref[...]
    ya = yg * jax.nn.sigmoid(z)
    ag = ag_ref[...].astype(F32)
    o_ref[...] = (ya * (ag * jax.nn.sigmoid(ag))).astype(o_ref.dtype)


def _glu(y_oct, h_rest, glu_w, glu_b):
    n_tok = y_oct.shape[1] * S5_CHUNK
    return pl.pallas_call(
        _glu_kernel,
        grid=(n_tok // GLU_TM,),
        in_specs=[pl.BlockSpec((S5_OCTETS, GLU_TM // S5_CHUNK, S5_FEAT), lambda i: (0, i, 0)),
                  pl.BlockSpec((GLU_TM, SSM_WIDTH), lambda i: (i, 0)),
                  pl.BlockSpec((SSM_WIDTH, SSM_WIDTH), lambda i: (0, 0)),
                  pl.BlockSpec((1, SSM_WIDTH), lambda i: (0, 0))],
        out_specs=pl.BlockSpec((GLU_TM, SSM_WIDTH), lambda i: (i, 0)),
        out_shape=jax.ShapeDtypeStruct((n_tok, SSM_WIDTH), BF16),
        scratch_shapes=[pltpu.VMEM((S5_OCTETS, GLU_TM, LANES), F32)],
        compiler_params=_cparams(("parallel",)),
        name="s5_glu",
    )(y_oct, h_rest, glu_w, glu_b)


GQA_TQ = 512
GQA_TK = 512
GQA_Q_BLK0 = 2
GQA_BG_BLK0 = 5
GQA_K_BLK0 = 16
GQA_V_BLK0 = 18


def _gqa_kernel(q_ref, k_ref, v_ref, bg_ref, o_ref, qs_ref, kt_ref, va_ref, m_ref, acc_ref, *, t_len):
    tq = GQA_TQ
    n_chunks = t_len // GQA_TK

    @pl.when(pl.program_id(2) == 0)
    def _():
        lane = lax.broadcasted_iota(jnp.int32, (t_len, HEAD_DIM), 1)
        va_ref[:, :HEAD_DIM] = v_ref[...]
        va_ref[:, HEAD_DIM:] = jnp.where(lane == 0, 1.0, 0.0).astype(BF16)
        for c in range(n_chunks):
            cols = slice(c * GQA_TK, (c + 1) * GQA_TK)
            kt_ref[:, cols] = k_ref[cols, :].T

    for g in range(B_GROUP):
        qs_ref[g * tq:(g + 1) * tq, :] = q_ref[:, g * HEAD_DIM:(g + 1) * HEAD_DIM]
    m_ref[...] = jnp.full(m_ref.shape, -jnp.inf, F32)
    acc_ref[...] = jnp.zeros(acc_ref.shape, F32)

    for c in range(n_chunks):
        cols = slice(c * GQA_TK, (c + 1) * GQA_TK)
        s = jnp.dot(qs_ref[...], kt_ref[:, cols], preferred_element_type=F32)
        m_old = m_ref[...]
        m_new = jnp.maximum(m_old, jnp.max(s, axis=-1, keepdims=True))
        p = jnp.exp2(s - jnp.concatenate([m_new] * (GQA_TK // LANES), axis=1))
        alpha = jnp.exp2(m_old - m_new)
        acc_ref[...] = (jnp.concatenate([alpha, alpha], axis=1) * acc_ref[...]
                        + jnp.dot(p.astype(BF16), va_ref[cols, :], preferred_element_type=F32))
        m_ref[...] = m_new

    acc = acc_ref[...]
    o = acc[:, :HEAD_DIM] / acc[:, HEAD_DIM:HEAD_DIM + 1]
    for g in range(B_GROUP):
        sl = slice(g * HEAD_DIM, (g + 1) * HEAD_DIM)
        bg = bg_ref[:, sl].astype(F32)
        o_ref[:, sl] = (o[g * tq:(g + 1) * tq, :] * (bg * jax.nn.sigmoid(bg))).astype(o_ref.dtype)


def _gqa(h_rest, bsz, t_len):
    n_tok = h_rest.shape[0]
    nq = t_len // GQA_TQ
    rows = B_GROUP * GQA_TQ
    return pl.pallas_call(
        functools.partial(_gqa_kernel, t_len=t_len),
        grid=(bsz, B_KV_HEADS, nq),
        in_specs=[pl.BlockSpec((GQA_TQ, B_GROUP * HEAD_DIM), lambda b, h, i: (b * nq + i, GQA_Q_BLK0 + h)),
                  pl.BlockSpec((t_len, HEAD_DIM), lambda b, h, i: (b, GQA_K_BLK0 + h)),
                  pl.BlockSpec((t_len, HEAD_DIM), lambda b, h, i: (b, GQA_V_BLK0 + h)),
                  pl.BlockSpec((GQA_TQ, B_GROUP * HEAD_DIM), lambda b, h, i: (b * nq + i, GQA_BG_BLK0 + h))],
        out_specs=pl.BlockSpec((GQA_TQ, B_GROUP * HEAD_DIM), lambda b, h, i: (b * nq + i, h)),
        out_shape=jax.ShapeDtypeStruct((n_tok, ATT_WIDTH), BF16),
        scratch_shapes=[pltpu.VMEM((rows, HEAD_DIM), BF16),
                        pltpu.VMEM((HEAD_DIM, t_len), BF16),
                        pltpu.VMEM((t_len, 2 * HEAD_DIM), BF16),
                        pltpu.VMEM((rows, LANES), F32),
                        pltpu.VMEM((rows, 2 * HEAD_DIM), F32)],
        compiler_params=_cparams(("parallel", "parallel", "arbitrary")),
        name="gqa_attn",
    )(h_rest, h_rest, h_rest, h_rest)


DIL_BLOCK = 128
DIL_WIN = DIL_BLOCK + 2 * C_HALF_SPAN
DIL_TQ_MAX = 512


def _dil_kernel(*refs, tq, n_sub, has_prev, is_last):
    it = iter(refs)
    q_ref = next(it)
    km_ref, kb_ref, ka_ref = next(it), next(it), next(it)
    vm_ref, vb_ref, va_ref = next(it), next(it), next(it)
    po_ref = next(it) if has_prev else None
    pl_ref = next(it) if has_prev else None
    g_ref = next(it) if is_last else None
    o_ref = next(it)
    lse_ref = None if is_last else next(it)
    acc_scr = next(it)
    oscr = None if is_last else next(it)
    lscr = None if is_last else next(it)

    j = pl.program_id(2)
    hs = C_HALF_SPAN
    qi = lax.broadcasted_iota(jnp.int32, (DIL_BLOCK, DIL_WIN), 0)
    ci = lax.broadcasted_iota(jnp.int32, (DIL_BLOCK, DIL_WIN), 1)
    rel = ci - qi
    band = (rel >= 0) & (rel <= 2 * hs)
    lane = lax.broadcasted_iota(jnp.int32, (DIL_BLOCK, LANES), 1)
    ones = jnp.ones((DIL_WIN, HEAD_DIM), BF16)

    def window(main_ref, before_ref, after_ref, r0, sl):
        lo, hi = r0 - hs, r0 + DIL_BLOCK + hs
        parts = []
        if lo < 0:
            parts.append(before_ref[0, 0, hs + lo:hs, sl])
        parts.append(main_ref[0, 0, max(lo, 0):min(hi, tq), sl])
        if hi > tq:
            parts.append(after_ref[0, 0, 0:hi - tq, sl])
        return parts[0] if len(parts) == 1 else jnp.concatenate(parts, axis=0)

    for i in range(tq // DIL_BLOCK):
        r0 = i * DIL_BLOCK
        rows = slice(r0, r0 + DIL_BLOCK)
        pos = ci + (j * tq + r0 - hs)
        mask = band & (pos >= 0) & (pos < n_sub)
        m_tile = jnp.zeros((DIL_BLOCK, LANES), F32)
        l_tile = jnp.ones((DIL_BLOCK, LANES), F32)
        for h in range(C_HEADS):
            sl = slice(h * HEAD_DIM, (h + 1) * HEAD_DIM)
            qb = q_ref[0, 0, rows, sl]
            kw = window(km_ref, kb_ref, ka_ref, r0, sl)
            vw = jnp.concatenate([window(vm_ref, vb_ref, va_ref, r0, sl), ones], axis=1)
            s = lax.dot_general(qb, kw, (((1,), (1,)), ((), ())), preferred_element_type=F32)
            s = jnp.where(mask, s, -jnp.inf)
            m = jnp.max(s, axis=-1, keepdims=True)
            p = jnp.exp2(s - m)
            pv = jnp.dot(p.astype(BF16), vw, preferred_element_type=F32)
            acc_scr[rows, sl] = pv[:, :HEAD_DIM]
            m_tile = jnp.where(lane == h, m, m_tile)
            l_tile = jnp.where(lane == h, pv[:, HEAD_DIM:], l_tile)
        lse_t = m_tile + jnp.log2(l_tile)
        if has_prev:
            lp_t = pl_ref[0, 0, rows, :]
            mx = jnp.maximum(lp_t, lse_t)
            e1 = jnp.exp2(lp_t - mx)
            e2 = jnp.exp2(lse_t - mx)
            den = e1 + e2
            w_prev = e1 / den
            w_cur = e2 / (den * l_tile)
            lse_t = mx + jnp.log2(den)
        else:
            w_prev = None
            w_cur = 1.0 / l_tile
        for h in range(C_HEADS):
            sl = slice(h * HEAD_DIM, (h + 1) * HEAD_DIM)
            o = acc_scr[rows, sl] * w_cur[:, h:h + 1]
            if has_prev:
                o = o + po_ref[0, 0, rows, sl].astype(F32) * w_prev[:, h:h + 1]
            if is_last:
                g = g_ref[0, 0, rows, sl].astype(F32)
                o_ref[0, 0, rows, sl] = (o * (g * jax.nn.sigmoid(g))).astype(o_ref.dtype)
            else:
                oscr[h, rows, :] = o
        if not is_last:
            lscr[rows, :] = lse_t

    if not is_last:
        sub = tq // DIL_FAN
        for kap in range(DIL_FAN):
            for h in range(C_HEADS):
                o_ref[0, 0, kap, :, h * HEAD_DIM:(h + 1) * HEAD_DIM] = (
                    oscr[h, pl.ds(kap, sub, stride=DIL_FAN), :].astype(o_ref.dtype))
            lse_ref[0, 0, kap] = lscr[pl.ds(kap, sub, stride=DIL_FAN), :]


def _dil_pass(src, bsz, t_len, d, prev):
    n_sub = t_len // d
    tq = min(DIL_TQ_MAX, n_sub)
    nt = n_sub // tq
    hs = C_HALF_SPAN
    hb = tq // hs
    last_halo = n_sub // hs - 1
    width = C_HEADS * HEAD_DIM
    f = DIL_FAN
    is_last = d == C_DILATIONS[-1]
    has_prev = prev is not None

    def main(sec, w=width):
        return pl.BlockSpec((1, 1, tq, w), lambda b, r, j: (b, r, j, sec))

    def before(sec):
        return pl.BlockSpec((1, 1, hs, width), lambda b, r, j: (b, r, jnp.maximum(j * hb - 1, 0), sec))

    def after(sec):
        return pl.BlockSpec((1, 1, hs, width), lambda b, r, j: (b, r, jnp.minimum((j + 1) * hb, last_halo), sec))

    in_specs = [main(0), main(1), before(1), after(1), main(2), before(2), after(2)]
    args = [src] * 7
    if has_prev:
        in_specs += [main(0), main(0, LANES)]
        args += list(prev)
    if is_last:
        in_specs.append(main(3))
        args.append(src)
    scratch = [pltpu.VMEM((tq, width), F32)]
    if is_last:
        out_specs = main(0)
        out_shape = jax.ShapeDtypeStruct((bsz, d, n_sub, width), BF16)
    else:
        out_specs = [pl.BlockSpec((1, 1, f, tq // f, width), lambda b, r, j: (b, r, 0, j, 0)),
                     pl.BlockSpec((1, 1, f, tq // f, LANES), lambda b, r, j: (b, r, 0, j, 0))]
        out_shape = [jax.ShapeDtypeStruct((bsz, d, f, n_sub // f, width), BF16),
                     jax.ShapeDtypeStruct((bsz, d, f, n_sub // f, LANES), F32)]
        scratch += [pltpu.VMEM((C_HEADS, tq, HEAD_DIM), F32), pltpu.VMEM((tq, LANES), F32)]
    res = pl.pallas_call(
        functools.partial(_dil_kernel, tq=tq, n_sub=n_sub, has_prev=has_prev, is_last=is_last),
        grid=(bsz, d, nt),
        in_specs=in_specs,
        out_specs=out_specs,
        out_shape=out_shape,
        scratch_shapes=scratch,
        compiler_params=_cparams(("parallel", "parallel", "arbitrary")),
        name=f"dilated_d{d}",
    )(*args)
    if is_last:
        return res
    return (res[0].reshape(bsz, d * f, n_sub // f, width), res[1].reshape(bsz, d * f, n_sub // f, LANES))


def _trunk(x, params):
    bsz, t_len, _ = x.shape
    x2d = x.reshape(bsz * t_len, D_MODEL)
    p = params
    au = _proj_au(x2d, p["w_au"])
    h_rest = _proj0(x2d, p["w_rest"], p["ax_cos"][t_len], p["ax_sin"][t_len], p["qn_g"], p["kn_g"], t_len)
    y_oct = _s5_scan(au, p["s5_w_in"], p["s5_w_x"], p["s5_w_out"], p["s5_tab"], t_len)
    mix_a = _glu(y_oct, h_rest, p["glu_w"], p["glu_b"])
    mix_b = _gqa(h_rest, bsz, t_len)
    x1, x1b = _out_ln0(mix_a, mix_b, p["even_w_out"], x2d, p["ln_g0"], p["ln_b0"])
    h1, h4, h16 = _proj1(x1b, p["odd_w_in"], p["pr_cos"][t_len], p["pr_sin"][t_len], bsz, t_len)
    prev = _dil_pass(h1.reshape(bsz, 1, t_len, h1.shape[1]), bsz, t_len, 1, None)
    prev = _dil_pass(h4, bsz, t_len, DIL_FAN, prev)
    n16 = t_len // OUT_CLASSES
    mix16 = _dil_pass(h16.reshape(bsz, OUT_CLASSES, n16, h16.shape[-1]), bsz, t_len, OUT_CLASSES, prev)
    x2 = _out_ln1(mix16, p["odd_w_out"], x1, p["ln_g1"], p["ln_b1"], bsz, t_len)
    return x2.reshape(bsz, t_len, D_MODEL)


def kernel(x_prompt, x_sample, even_w_in, even_w_out, ssm_a_re, ssm_a_im, ssm_log_dt, ssm_b_re, ssm_b_im, ssm_c_re, ssm_c_im, ssm_d, ssm_glu_w, ssm_glu_b, attn_q_norm, attn_k_norm, odd_w_in, odd_w_out, ln_g, ln_b):
    c_in, c_x, c_out, tab = _s5_tables(ssm_a_re[0], ssm_a_im[0], ssm_log_dt[0], ssm_b_re[0], ssm_b_im[0],
                                       ssm_c_re[0], ssm_c_im[0], ssm_d[0])
    w_in = _s5_expand(c_in, SSM_STATE, SSM_GROUP)
    w_x = _s5_expand(c_x, SSM_GROUP, SSM_GROUP)
    w_out = _s5_expand(c_out, SSM_GROUP, SSM_STATE)
    lens = sorted({x_prompt.shape[1], x_sample.shape[1]})
    ax = {t: _axial_tables(t) for t in lens}
    pr = {t: _partial_tables(t) for t in lens}
    params = {
        "w_au": even_w_in[0][:, :SSM_WIDTH].astype(BF16),
        "w_rest": even_w_in[0][:, SSM_WIDTH:].astype(BF16),
        "even_w_out": even_w_out[0].astype(BF16),
        "odd_w_in": odd_w_in[0].astype(BF16),
        "odd_w_out": odd_w_out[0].astype(BF16),
        "glu_w": ssm_glu_w[0].astype(BF16),
        "glu_b": ssm_glu_b[0].astype(F32).reshape(1, SSM_WIDTH),
        "qn_g": attn_q_norm[0].astype(F32).reshape(1, HEAD_DIM),
        "kn_g": attn_k_norm[0].astype(F32).reshape(1, HEAD_DIM),
        "ln_g0": ln_g[0].astype(F32).reshape(1, D_MODEL),
        "ln_b0": ln_b[0].astype(F32).reshape(1, D_MODEL),
        "ln_g1": ln_g[1].astype(F32).reshape(1, D_MODEL),
        "ln_b1": ln_b[1].astype(F32).reshape(1, D_MODEL),
        "s5_w_in": w_in, "s5_w_x": w_x, "s5_w_out": w_out, "s5_tab": tab,
        "ax_cos": {t: ax[t][0] for t in lens}, "ax_sin": {t: ax[t][1] for t in lens},
        "pr_cos": {t: pr[t][0] for t in lens}, "pr_sin": {t: pr[t][1] for t in lens},
    }
    return (_trunk(x_prompt, params), _trunk(x_sample, params))
```

```python
import functools
import math

import jax
import jax.numpy as jnp
from jax import lax
from jax.experimental import pallas as pl
from jax.experimental.pallas import tpu as pltpu

F32 = jnp.float32
BF16 = jnp.bfloat16

D_MODEL = 2048
HEAD_DIM = 128
GRID_W = 64
SSM_WIDTH = 1024
SSM_GROUP = 16
SSM_GROUPS = 64
SSM_STATE = 64
ATT_WIDTH = 1024
B_KV_HEADS = 2
B_GROUP = 4
KV_WIDTH = B_KV_HEADS * HEAD_DIM
AXIAL_THETA = 10000.0
C_HEADS = 16
C_DILATIONS = (1, 4, 16)
C_HALF_SPAN = 64
ROPE_THETA = 500000.0
ROPE_DIMS = 32
DEPTH = 2
DN_ALPHA = (2 * DEPTH) ** 0.25
LN_EPS = 1e-5
QK_EPS = 1e-6
ATT_SCALE = HEAD_DIM ** -0.5
LOG2E = math.log2(math.e)

LANES = 128
SUBLANES = 8
VMEM_LIMIT_BYTES = 56 * 1024 * 1024

S5_CHUNK = 16
S5_OCTETS = SSM_GROUPS // 8
S5_FEAT = S5_CHUNK * LANES
S5_HALF = 8 * SSM_STATE
S5_COMPACT = S5_FEAT // 8
S5_LAG_COLS = 4 * LANES
S5_ROWS = 512


def _cparams(sem):
    return pltpu.CompilerParams(dimension_semantics=sem, vmem_limit_bytes=VMEM_LIMIT_BYTES)


def _axial_tables(t_len):
    pos = jnp.arange(t_len, dtype=jnp.int32)
    row = (pos // GRID_W).astype(F32)
    col = (pos % GRID_W).astype(F32)
    n = HEAD_DIM // 2
    inv = AXIAL_THETA ** (-jnp.arange(0, n, 2, dtype=F32) / n)
    ar = row[:, None] * inv[None, :]
    ac = col[:, None] * inv[None, :]
    cos = jnp.concatenate([jnp.cos(ar), jnp.cos(ar), jnp.cos(ac), jnp.cos(ac)], axis=-1)
    sin = jnp.concatenate([-jnp.sin(ar), jnp.sin(ar), -jnp.sin(ac), jnp.sin(ac)], axis=-1)
    return cos, sin


def _partial_tables(t_len):
    pos = jnp.arange(t_len, dtype=F32)
    n = ROPE_DIMS
    inv = ROPE_THETA ** (-jnp.arange(0, n, 2, dtype=F32) / n)
    ang = pos[:, None] * inv[None, :]
    rest = HEAD_DIM - n
    cos = jnp.concatenate([jnp.cos(ang), jnp.cos(ang), jnp.ones((t_len, rest), F32)], axis=-1)
    sin = jnp.concatenate([-jnp.sin(ang), jnp.sin(ang), jnp.zeros((t_len, rest), F32)], axis=-1)
    return cos, sin


def _rotate_pairs(x, cos, sin, half, period):
    lane = lax.broadcasted_iota(jnp.int32, x.shape, x.ndim - 1)
    up = pltpu.roll(x, LANES - half, x.ndim - 1)
    dn = pltpu.roll(x, half, x.ndim - 1)
    partner = jnp.where((lane % period) < half, up, dn)
    return x * cos + partner * sin


PROJ_TM = 1024


def _proj_au_kernel(x_ref, w_ref, o_ref, scr_ref):
    res = jnp.dot(x_ref[...].astype(BF16), w_ref[...], preferred_element_type=F32)
    rows = o_ref.shape[1]
    for g in range(S5_OCTETS):
        scr_ref[...] = res[:, g * LANES:(g + 1) * LANES]
        for s in range(S5_CHUNK):
            o_ref[g, :, s * LANES:(s + 1) * LANES] = scr_ref[pl.ds(s, rows, stride=S5_CHUNK), :].astype(o_ref.dtype)


PROJ_AU_TM = 512


def _proj_au(x2d, w_au):
    n_tok = x2d.shape[0]
    tm = PROJ_AU_TM
    return pl.pallas_call(
        _proj_au_kernel,
        grid=(n_tok // tm,),
        in_specs=[pl.BlockSpec((tm, D_MODEL), lambda i: (i, 0)),
                  pl.BlockSpec((D_MODEL, SSM_WIDTH), lambda i: (0, 0))],
        out_specs=pl.BlockSpec((S5_OCTETS, tm // S5_CHUNK, S5_FEAT), lambda i: (0, i, 0)),
        out_shape=jax.ShapeDtypeStruct((S5_OCTETS, n_tok // S5_CHUNK, S5_FEAT), BF16),
        scratch_shapes=[pltpu.VMEM((tm, LANES), F32)],
        compiler_params=_cparams(("parallel",)),
        name="proj_au",
    )(x2d, w_au)


P0_TN = 512
P0_Q_BLOCKS = (2, 3)
P0_KV_BLOCK = 4
P0_ROW_PIECES = 4


def _proj0_kernel(x_ref, w_ref, cos_ref, sin_ref, qg_ref, kg_ref, o_ref, xb_ref):
    j = pl.program_id(1)

    @pl.when(j == 0)
    def _():
        xb_ref[...] = x_ref[...].astype(BF16)

    is_q = (j == P0_Q_BLOCKS[0]) | (j == P0_Q_BLOCKS[1])
    is_kv = j == P0_KV_BLOCK
    gain = jnp.where(is_q, qg_ref[...], kg_ref[...])
    scale = jnp.where(is_q, ATT_SCALE * LOG2E, 1.0).astype(F32)
    tm = xb_ref.shape[0]
    piece = tm // P0_ROW_PIECES
    for r in range(P0_ROW_PIECES):
        rows = slice(r * piece, (r + 1) * piece)
        res = jnp.dot(xb_ref[rows, :], w_ref[...], preferred_element_type=F32)
        for h in range(P0_TN // HEAD_DIM):
            sl = slice(h * HEAD_DIM, (h + 1) * HEAD_DIM)
            normed = (is_q | is_kv) if h < B_KV_HEADS else is_q
            x = res[:, sl]
            ms = jnp.mean(x * x, axis=-1, keepdims=True)
            xn = x * jnp.where(normed, lax.rsqrt(ms + QK_EPS), 1.0) * jnp.where(normed, gain, 1.0)
            cos = jnp.where(normed, cos_ref[rows, :], 1.0)
            sin = jnp.where(normed, sin_ref[rows, :], 0.0)
            val = _rotate_pairs(xn, cos, sin, HEAD_DIM // 4, HEAD_DIM // 2) * scale
            o_ref[rows, sl] = val.astype(o_ref.dtype)


def _proj0(x2d, w_rest, cos, sin, qn_g, kn_g, t_len):
    n_tok = x2d.shape[0]
    n_cols = w_rest.shape[1]
    tm = min(PROJ_TM, t_len)
    tpb = t_len // tm
    return pl.pallas_call(
        _proj0_kernel,
        grid=(n_tok // tm, n_cols // P0_TN),
        in_specs=[pl.BlockSpec((tm, D_MODEL), lambda i, j: (i, 0)),
                  pl.BlockSpec((D_MODEL, P0_TN), lambda i, j: (0, j)),
                  pl.BlockSpec((tm, HEAD_DIM), lambda i, j: (i % tpb, 0)),
                  pl.BlockSpec((tm, HEAD_DIM), lambda i, j: (i % tpb, 0)),
                  pl.BlockSpec((1, HEAD_DIM), lambda i, j: (0, 0)),
                  pl.BlockSpec((1, HEAD_DIM), lambda i, j: (0, 0))],
        out_specs=pl.BlockSpec((tm, P0_TN), lambda i, j: (i, j)),
        out_shape=jax.ShapeDtypeStruct((n_tok, n_cols), BF16),
        scratch_shapes=[pltpu.VMEM((tm, D_MODEL), BF16)],
        compiler_params=_cparams(("parallel", "arbitrary")),
        name="proj0",
    )(x2d, w_rest, cos, sin, qn_g, kn_g)


P1_TM = 2048
P1_TN = 512
P1_ROT_BLOCKS = 2 * (C_HEADS * HEAD_DIM) // P1_TN
P1_Q_BLOCKS = (C_HEADS * HEAD_DIM) // P1_TN
P1_ROW_PIECES = 4
DIL_FAN = 4


def _proj1_kernel(x_ref, w_ref, cos_ref, sin_ref, o_ref, o4_ref, o16_ref, rs_ref, r4_ref):
    j = pl.program_id(1)
    heads = P1_TN // HEAD_DIM
    slabs = [slice(h * HEAD_DIM, (h + 1) * HEAD_DIM) for h in range(heads)]

    rotated = j < P1_ROT_BLOCKS
    scale = jnp.where(j < P1_Q_BLOCKS, ATT_SCALE * LOG2E, 1.0).astype(F32)
    piece = rs_ref.shape[1] // P1_ROW_PIECES
    r4 = piece // DIL_FAN
    r16 = piece // (DIL_FAN * DIL_FAN)
    for r in range(P1_ROW_PIECES):
        rows = slice(r * piece, (r + 1) * piece)
        res = jnp.dot(x_ref[rows, :], w_ref[...], preferred_element_type=F32)
        cos = jnp.where(rotated, cos_ref[rows, :], 1.0)
        sin = jnp.where(rotated, sin_ref[rows, :], 0.0)
        for h, sl in enumerate(slabs):
            val = _rotate_pairs(res[:, sl], cos, sin, ROPE_DIMS // 2, HEAD_DIM) * scale
            o_ref[rows, sl] = val.astype(o_ref.dtype)
            rs_ref[h, rows, :] = val
            for rho in range(DIL_FAN):
                cls = rs_ref[h, pl.ds(r * piece + rho, r4, stride=DIL_FAN), :]
                r4_ref[h, r * piece + rho * r4:r * piece + (rho + 1) * r4, :] = cls
                o4_ref[0, rho, r * r4:(r + 1) * r4, sl] = cls.astype(o4_ref.dtype)
            for rho in range(DIL_FAN):
                for kap in range(DIL_FAN):
                    o16_ref[0, rho, kap, r * r16:(r + 1) * r16, sl] = r4_ref[
                        h, pl.ds(r * piece + rho * r4 + kap, r16, stride=DIL_FAN), :].astype(o16_ref.dtype)


def _proj1(xb2d, w, cos, sin, bsz, t_len):
    n_tok = xb2d.shape[0]
    n_cols = w.shape[1]
    tm = min(P1_TM, t_len)
    tpb = t_len // tm
    f = DIL_FAN
    return pl.pallas_call(
        _proj1_kernel,
        grid=(n_tok // tm, n_cols // P1_TN),
        in_specs=[pl.BlockSpec((tm, D_MODEL), lambda i, j: (i, 0)),
                  pl.BlockSpec((D_MODEL, P1_TN), lambda i, j: (0, j)),
                  pl.BlockSpec((tm, HEAD_DIM), lambda i, j: (i % tpb, 0)),
                  pl.BlockSpec((tm, HEAD_DIM), lambda i, j: (i % tpb, 0))],
        out_specs=[pl.BlockSpec((tm, P1_TN), lambda i, j: (i, j)),
                   pl.BlockSpec((1, f, tm // f, P1_TN), lambda i, j: (i // tpb, 0, i % tpb, j)),
                   pl.BlockSpec((1, f, f, tm // (f * f), P1_TN), lambda i, j: (i // tpb, 0, 0, i % tpb, j))],
        out_shape=[jax.ShapeDtypeStruct((n_tok, n_cols), BF16),
                   jax.ShapeDtypeStruct((bsz, f, t_len // f, n_cols), BF16),
                   jax.ShapeDtypeStruct((bsz, f, f, t_len // (f * f), n_cols), BF16)],
        scratch_shapes=[pltpu.VMEM((P1_TN // HEAD_DIM, tm, HEAD_DIM), F32),
                        pltpu.VMEM((P1_TN // HEAD_DIM, tm, HEAD_DIM), F32)],
        compiler_params=_cparams(("parallel", "arbitrary")),
        name="proj1",
    )(xb2d, w, cos, sin)


OUT_TM = 512
OUT_ROW_PIECES = 2


def _residual_ln(x, y, g, b):
    z = DN_ALPHA * x + y
    mu = jnp.mean(z, axis=-1, keepdims=True)
    zc = z - mu
    var = jnp.mean(zc * zc, axis=-1, keepdims=True)
    return zc * lax.rsqrt(var + LN_EPS) * g + b


def _out_ln0_kernel(ma_ref, mb_ref, w_ref, x_ref, g_ref, b_ref, o_ref, ob_ref):
    half = ma_ref.shape[1]
    piece = OUT_TM // OUT_ROW_PIECES
    for r in range(OUT_ROW_PIECES):
        rows = slice(r * piece, (r + 1) * piece)
        y = jnp.dot(ma_ref[rows, :], w_ref[:half, :], preferred_element_type=F32)
        y = y + jnp.dot(mb_ref[rows, :], w_ref[half:, :], preferred_element_type=F32)
        out = _residual_ln(x_ref[rows, :], y, g_ref[...], b_ref[...])
        o_ref[rows, :] = out
        ob_ref[rows, :] = out.astype(ob_ref.dtype)


def _out_ln0(mix_a, mix_b, w_out, x2d, ln_g, ln_b):
    n_tok = x2d.shape[0]
    half = D_MODEL // 2
    row = lambda i: (i, 0)
    fixed = lambda i: (0, 0)
    return pl.pallas_call(
        _out_ln0_kernel,
        grid=(n_tok // OUT_TM,),
        in_specs=[pl.BlockSpec((OUT_TM, half), row),
                  pl.BlockSpec((OUT_TM, half), row),
                  pl.BlockSpec((D_MODEL, D_MODEL), fixed),
                  pl.BlockSpec((OUT_TM, D_MODEL), row),
                  pl.BlockSpec((1, D_MODEL), fixed),
                  pl.BlockSpec((1, D_MODEL), fixed)],
        out_specs=[pl.BlockSpec((OUT_TM, D_MODEL), row), pl.BlockSpec((OUT_TM, D_MODEL), row)],
        out_shape=[jax.ShapeDtypeStruct((n_tok, D_MODEL), F32), jax.ShapeDtypeStruct((n_tok, D_MODEL), BF16)],
        compiler_params=_cparams(("parallel",)),
        name="out_ln0",
    )(mix_a, mix_b, w_out, x2d, ln_g, ln_b)


OUT_CLASSES = DIL_FAN * DIL_FAN
OUT_PER_CLASS = OUT_TM // OUT_CLASSES


def _out_ln1_kernel(m_ref, p_ref, w_ref, x_ref, g_ref, b_ref, o_ref, ms_ref):
    piece = OUT_TM // OUT_ROW_PIECES
    per = piece // OUT_CLASSES
    for r in range(OUT_ROW_PIECES):
        rows = slice(r * piece, (r + 1) * piece)
        for idx in range(OUT_CLASSES):
            ms_ref[r * piece + idx * per:r * piece + (idx + 1) * per, :] = m_ref[0, idx, r * per:(r + 1) * per, :]
        m_tok = jnp.dot(p_ref[...], ms_ref[rows, :], preferred_element_type=F32).astype(BF16)
        y = jnp.dot(m_tok, w_ref[...], preferred_element_type=F32)
        o_ref[rows, :] = _residual_ln(x_ref[rows, :], y, g_ref[...], b_ref[...])


def _out_ln1(mix16, w_out, x2d, ln_g, ln_b, bsz, t_len):
    n_tok = x2d.shape[0]
    tpb = t_len // OUT_TM
    fixed = lambda i: (0, 0)
    piece = OUT_TM // OUT_ROW_PIECES
    per = piece // OUT_CLASSES
    tok = jnp.arange(piece)
    cls = tok % OUT_CLASSES
    src = ((cls % DIL_FAN) * DIL_FAN + cls // DIL_FAN) * per + tok // OUT_CLASSES
    perm = (src[:, None] == jnp.arange(piece)[None, :]).astype(BF16)
    return pl.pallas_call(
        _out_ln1_kernel,
        grid=(n_tok // OUT_TM,),
        in_specs=[pl.BlockSpec((1, OUT_CLASSES, OUT_PER_CLASS, D_MODEL), lambda i: (i // tpb, 0, i % tpb, 0)),
                  pl.BlockSpec((piece, piece), fixed),
                  pl.BlockSpec((D_MODEL, D_MODEL), fixed),
                  pl.BlockSpec((OUT_TM, D_MODEL), lambda i: (i, 0)),
                  pl.BlockSpec((1, D_MODEL), fixed),
                  pl.BlockSpec((1, D_MODEL), fixed)],
        out_specs=pl.BlockSpec((OUT_TM, D_MODEL), lambda i: (i, 0)),
        out_shape=jax.ShapeDtypeStruct((n_tok, D_MODEL), F32),
        scratch_shapes=[pltpu.VMEM((OUT_TM, D_MODEL), BF16)],
        compiler_params=_cparams(("parallel",)),
        name="out_ln1",
    )(mix16, perm, w_out, x2d, ln_g, ln_b)


def _s5_tables(a_re, a_im, log_dt, b_re, b_im, c_re, c_im, d_skip):
    hi = lax.Precision.HIGHEST
    ar = a_re.astype(F32)
    ai = a_im.astype(F32)
    dt = jnp.exp(log_dt.astype(F32))[..., None]

    def lam_pow(k):
        kk = k.astype(F32)[:, None, None, None]
        mag = jnp.exp(kk * (ar * dt)[None])
        ang = kk * (ai * dt)[None]
        return mag * jnp.cos(ang), mag * jnp.sin(ang)

    pw_re, pw_im = lam_pow(jnp.arange(S5_CHUNK + 1))
    lb_re, lb_im = pw_re[1], pw_im[1]
    nr = lb_re - 1.0
    den = jnp.square(ar) + jnp.square(ai)
    f_re = (nr * ar + lb_im * ai) / den
    f_im = (lb_im * ar - nr * ai) / den
    br = b_re.astype(F32)[None]
    bi = b_im.astype(F32)[None]
    bb_re = f_re[..., None] * br - f_im[..., None] * bi
    bb_im = f_re[..., None] * bi + f_im[..., None] * br
    cr = c_re.astype(F32)
    ci = c_im.astype(F32)

    L = S5_CHUNK


    s_idx = jnp.arange(L)

    def by_octet(x):
        return x.reshape(2, L, S5_OCTETS, 8, SSM_STATE)

    def in_pow(p):
        return by_octet(jnp.stack([p[L - 1 - s_idx, 0], p[s_idx, 1]])).transpose(2, 1, 3, 0, 4)[:, :, :, None]

    def in_b(x):
        return x.reshape(2, S5_OCTETS, 8, SSM_STATE, SSM_GROUP).transpose(1, 2, 4, 0, 3)[:, None]

    pr, pi, xr, xi = in_pow(pw_re), in_pow(pw_im), in_b(bb_re), in_b(bb_im)
    c_in = jnp.stack([pr * xr - pi * xi, pr * xi + pi * xr], axis=5)
    c_in = c_in.reshape(S5_OCTETS, S5_FEAT, S5_COMPACT)

    def out_pow(p):
        return by_octet(jnp.stack([p[s_idx + 1, 0], p[L - s_idx, 1]])).transpose(2, 0, 3, 4, 1)[..., None]

    def out_c(x):
        return x.reshape(2, S5_OCTETS, 8, SSM_GROUP, SSM_STATE).transpose(1, 0, 2, 4, 3)[:, :, :, :, None]

    qr, qi, yr, yi = out_pow(pw_re), out_pow(pw_im), out_c(cr), out_c(ci)
    c_out = jnp.stack([yr * qr - yi * qi, -(yr * qi + yi * qr)], axis=2)
    c_out = c_out.reshape(S5_OCTETS, S5_FEAT, S5_COMPACT)

    def k_lag(direction):
        p_re = pw_re[:L, direction]
        p_im = pw_im[:L, direction]
        cl_re = jnp.einsum('gop,kgp->kgop', cr[direction], p_re) - jnp.einsum('gop,kgp->kgop', ci[direction], p_im)
        cl_im = jnp.einsum('gop,kgp->kgop', cr[direction], p_im) + jnp.einsum('gop,kgp->kgop', ci[direction], p_re)
        return (jnp.einsum('kgop,gpi->gkio', cl_re, bb_re[direction], precision=hi)
                - jnp.einsum('kgop,gpi->gkio', cl_im, bb_im[direction], precision=hi))

    kf = k_lag(0)
    kr = k_lag(1)
    skip = jnp.eye(SSM_GROUP, dtype=F32)[None] * d_skip.astype(F32).reshape(SSM_GROUPS, 1, SSM_GROUP)
    centre = (kf[:, 0] + kr[:, 0] + skip)[:, None]
    k_lags = jnp.concatenate([kr[:, :0:-1], centre, kf[:, 1:]], axis=1)
    k_lags = k_lags.reshape(S5_OCTETS, 8, 2 * L - 1, SSM_GROUP, SSM_GROUP).transpose(0, 1, 3, 2, 4)
    k_lags = k_lags.reshape(S5_OCTETS, LANES, (2 * L - 1) * SSM_GROUP)
    c_x = jnp.pad(k_lags, ((0, 0), (0, 0), (0, S5_LAG_COLS - k_lags.shape[-1])))

    q_re, q_im = lam_pow(L * jnp.arange(1, 9))
    rows = jnp.arange(8)

    def lanes(x):
        return x.reshape(x.shape[:-2] + (S5_OCTETS, S5_HALF))

    def hs_tab(direction):
        out = []
        for part in (q_re, q_im):
            for sh in (1, 2, 4):
                coef = lanes(part[sh - 1, direction])
                mask = (rows >= sh) if direction == 0 else (rows <= 7 - sh)
                out.append(jnp.where(mask[None, :, None], coef[:, None, :], 0.0))
        for part in (q_re, q_im):
            idx = rows if direction == 0 else 7 - rows
            out.append(lanes(part[idx, direction]).transpose(1, 0, 2))
        return out

    tab = jnp.stack(hs_tab(0) + hs_tab(1), axis=1)
    return c_in.astype(BF16), c_x.astype(BF16), c_out.astype(BF16), tab


S5_EXPAND_ROWS = 256


def _s5_expand_kernel(c_ref, e_ref, o_ref, *, row_span, col_span):
    for r in range(S5_FEAT // S5_EXPAND_ROWS):
        rows = slice(r * S5_EXPAND_ROWS, (r + 1) * S5_EXPAND_ROWS)
        w = jnp.dot(c_ref[0, rows, :], e_ref[...], preferred_element_type=F32)
        ri = lax.broadcasted_iota(jnp.int32, w.shape, 0) + r * S5_EXPAND_ROWS
        ci = lax.broadcasted_iota(jnp.int32, w.shape, 1)
        same = ((ri % (8 * row_span)) // row_span) == ((ci % (8 * col_span)) // col_span)
        o_ref[0, rows, :] = jnp.where(same, w, 0.0).astype(o_ref.dtype)


def _s5_expand_lags_kernel(k_ref, e_ref, o_ref):
    ri = lax.broadcasted_iota(jnp.int32, (LANES, S5_FEAT), 0)
    ci = lax.broadcasted_iota(jnp.int32, (LANES, S5_FEAT), 1)
    same = (ri // SSM_GROUP) == ((ci % LANES) // SSM_GROUP)
    k_all = k_ref[0].astype(F32)
    for s in range(S5_CHUNK):
        off = (S5_CHUNK - 1 - s) * SSM_GROUP
        w = jnp.dot(k_all[:, off:off + S5_COMPACT].astype(BF16), e_ref[...], preferred_element_type=F32)
        o_ref[0, s * LANES:(s + 1) * LANES, :] = jnp.where(same, w, 0.0).astype(o_ref.dtype)


def _s5_expand_lags(k_lags):
    k = jnp.arange(S5_COMPACT)[:, None]
    c = jnp.arange(S5_FEAT)[None, :]
    spread = ((k // SSM_GROUP == c // LANES) & (k % SSM_GROUP == c % SSM_GROUP)).astype(BF16)
    return pl.pallas_call(
        _s5_expand_lags_kernel,
        grid=(S5_OCTETS,),
        in_specs=[pl.BlockSpec((1, LANES, S5_LAG_COLS), lambda o: (o, 0, 0)),
                  pl.BlockSpec((S5_COMPACT, S5_FEAT), lambda o: (0, 0))],
        out_specs=pl.BlockSpec((1, S5_FEAT, S5_FEAT), lambda o: (o, 0, 0)),
        out_shape=jax.ShapeDtypeStruct((S5_OCTETS, S5_FEAT, S5_FEAT), BF16),
        compiler_params=_cparams(("parallel",)),
        name="s5_expand_lags",
    )(k_lags, spread)


def _s5_expand(compact, col_span, row_span):
    k = jnp.arange(S5_COMPACT)[:, None]
    c = jnp.arange(S5_FEAT)[None, :]
    spread = ((k // col_span == c // (8 * col_span)) & (k % col_span == c % col_span)).astype(BF16)
    return pl.pallas_call(
        functools.partial(_s5_expand_kernel, row_span=row_span, col_span=col_span),
        grid=(S5_OCTETS,),
        in_specs=[pl.BlockSpec((1, S5_FEAT, S5_COMPACT), lambda o: (o, 0, 0)),
                  pl.BlockSpec((S5_COMPACT, S5_FEAT), lambda o: (0, 0))],
        out_specs=pl.BlockSpec((1, S5_FEAT, S5_FEAT), lambda o: (o, 0, 0)),
        out_shape=jax.ShapeDtypeStruct((S5_OCTETS, S5_FEAT, S5_FEAT), BF16),
        compiler_params=_cparams(("parallel",)),
        name="s5_expand",
    )(compact, spread)


def _s5_kernel(u_ref, wi_ref, wx_ref, wo_ref, tab_ref, y_ref, s_ref, *, seqs, nchunk):
    u = u_ref[0]
    s_ref[...] = jnp.dot(u, wi_ref[0], preferred_element_type=F32)
    nv = nchunk // SUBLANES
    row = lax.broadcasted_iota(jnp.int32, (SUBLANES, S5_HALF), 0)
    zero = jnp.zeros((SUBLANES, S5_HALF), F32)
    H = S5_HALF

    def bcast(x, r):
        return jnp.broadcast_to(x[r:r + 1, :], x.shape)

    def scan_block(raw_re, raw_im, nb_raw_re, nb_raw_im, nb_e_re, nb_e_im, tab0, forward):
        edge, src = (0, SUBLANES - 1) if forward else (SUBLANES - 1, 0)
        step = 1 if forward else SUBLANES - 1
        x_re = jnp.where(row == edge, bcast(nb_raw_re, src), pltpu.roll(raw_re, step, 0))
        x_im = jnp.where(row == edge, bcast(nb_raw_im, src), pltpu.roll(raw_im, step, 0))
        for k, sh in enumerate((1, 2, 4)):
            c_re = tab_ref[0, tab0 + k]
            c_im = tab_ref[0, tab0 + 3 + k]
            amt = sh if forward else SUBLANES - sh
            s_re = pltpu.roll(x_re, amt, 0)
            s_im = pltpu.roll(x_im, amt, 0)
            x_re, x_im = x_re + c_re * s_re - c_im * s_im, x_im + c_re * s_im + c_im * s_re
        p_re = tab_ref[0, tab0 + 6]
        p_im = tab_ref[0, tab0 + 7]
        k_re = bcast(nb_e_re, src)
        k_im = bcast(nb_e_im, src)
        return x_re + p_re * k_re - p_im * k_im, x_im + p_re * k_im + p_im * k_re

    y_x = jnp.dot(u, wx_ref[0], preferred_element_type=F32)

    for b in range(seqs):
        base = b * nchunk
        f_raw_re = f_raw_im = f_e_re = f_e_im = r_raw_re = r_raw_im = r_e_re = r_e_im = zero
        for v in range(nv):
            fr = slice(base + v * SUBLANES, base + (v + 1) * SUBLANES)
            raw_re = s_ref[fr, 0:H]
            raw_im = s_ref[fr, H:2 * H]
            f_e_re, f_e_im = scan_block(raw_re, raw_im, f_raw_re, f_raw_im, f_e_re, f_e_im, 0, True)
            f_raw_re, f_raw_im = raw_re, raw_im
            s_ref[fr, 0:H] = f_e_re
            s_ref[fr, H:2 * H] = f_e_im
            rr = slice(base + (nv - 1 - v) * SUBLANES, base + (nv - v) * SUBLANES)
            rraw_re = s_ref[rr, 2 * H:3 * H]
            rraw_im = s_ref[rr, 3 * H:4 * H]
            r_e_re, r_e_im = scan_block(rraw_re, rraw_im, r_raw_re, r_raw_im, r_e_re, r_e_im, 8, False)
            r_raw_re, r_raw_im = rraw_re, rraw_im
            s_ref[rr, 2 * H:3 * H] = r_e_re
            s_ref[rr, 3 * H:4 * H] = r_e_im

    y = y_x + jnp.dot(s_ref[...].astype(BF16), wo_ref[0], preferred_element_type=F32)
    y_ref[0] = y.astype(y_ref.dtype)


def _s5_scan(u, w_in, w_x, w_out, tab, t_len):
    rows = u.shape[1]
    nchunk = t_len // S5_CHUNK
    seqs = S5_ROWS // nchunk
    wspec = pl.BlockSpec((1, S5_FEAT, S5_FEAT), lambda o, i: (o, 0, 0), pipeline_mode=pl.Buffered(1))
    return pl.pallas_call(
        functools.partial(_s5_kernel, seqs=seqs, nchunk=nchunk),
        grid=(S5_OCTETS, rows // S5_ROWS),
        in_specs=[pl.BlockSpec((1, S5_ROWS, S5_FEAT), lambda o, i: (o, i, 0)),
                  wspec, wspec, wspec,
                  pl.BlockSpec((1, 16, SUBLANES, S5_HALF), lambda o, i: (o, 0, 0, 0))],
        out_specs=pl.BlockSpec((1, S5_ROWS, S5_FEAT), lambda o, i: (o, i, 0)),
        out_shape=jax.ShapeDtypeStruct((S5_OCTETS, rows, S5_FEAT), BF16),
        scratch_shapes=[pltpu.VMEM((S5_ROWS, 4 * S5_HALF), F32)],
        compiler_params=_cparams(("arbitrary", "arbitrary")),
        name="s5_scan",
    )(u, w_in, w_x, w_out, tab)


GLU_TM = 512


def _glu_kernel(y_ref, ag_ref, w_ref, b_ref, o_ref, ys_ref):
    rows = y_ref.shape[1]
    for g in range(S5_OCTETS):
        for s in range(S5_CHUNK):
            ys_ref[g, pl.ds(s, rows, stride=S5_CHUNK), :] = y_ref[g, :, s * LANES:(s + 1) * LANES].astype(F32)
    y = jnp.concatenate([ys_ref[g] for g in range(S5_OCTETS)], axis=-1)
    yg = jax.nn.gelu(y)
    z = jnp.dot(yg.astype(BF16), w_ref[...], preferred_element_type=F32) + b_ref[...]
    ya = yg * jax.nn.sigmoid(z)
    ag = ag_ref[...].astype(F32)
    o_ref[...] = (ya * (ag * jax.nn.sigmoid(ag))).astype(o_ref.dtype)


def _glu(y_oct, h_rest, glu_w, glu_b):
    n_tok = y_oct.shape[1] * S5_CHUNK
    return pl.pallas_call(
        _glu_kernel,
        grid=(n_tok // GLU_TM,),
        in_specs=[pl.BlockSpec((S5_OCTETS, GLU_TM // S5_CHUNK, S5_FEAT), lambda i: (0, i, 0)),
                  pl.BlockSpec((GLU_TM, SSM_WIDTH), lambda i: (i, 0)),
                  pl.BlockSpec((SSM_WIDTH, SSM_WIDTH), lambda i: (0, 0)),
                  pl.BlockSpec((1, SSM_WIDTH), lambda i: (0, 0))],
        out_specs=pl.BlockSpec((GLU_TM, SSM_WIDTH), lambda i: (i, 0)),
        out_shape=jax.ShapeDtypeStruct((n_tok, SSM_WIDTH), BF16),
        scratch_shapes=[pltpu.VMEM((S5_OCTETS, GLU_TM, LANES), F32)],
        compiler_params=_cparams(("parallel",)),
        name="s5_glu",
    )(y_oct, h_rest, glu_w, glu_b)


GQA_TQ = 512
GQA_TK = 512
GQA_Q_BLK0 = 2
GQA_BG_BLK0 = 5
GQA_K_BLK0 = 16
GQA_V_BLK0 = 18


def _gqa_kernel(q_ref, k_ref, v_ref, bg_ref, o_ref, qs_ref, kt_ref, va_ref, m_ref, acc_ref, *, t_len):
    tq = GQA_TQ
    n_chunks = t_len // GQA_TK

    @pl.when(pl.program_id(2) == 0)
    def _():
        lane = lax.broadcasted_iota(jnp.int32, (t_len, HEAD_DIM), 1)
        va_ref[:, :HEAD_DIM] = v_ref[...]
        va_ref[:, HEAD_DIM:] = jnp.where(lane == 0, 1.0, 0.0).astype(BF16)
        for c in range(n_chunks):
            cols = slice(c * GQA_TK, (c + 1) * GQA_TK)
            kt_ref[:, cols] = k_ref[cols, :].T

    for g in range(B_GROUP):
        qs_ref[g * tq:(g + 1) * tq, :] = q_ref[:, g * HEAD_DIM:(g + 1) * HEAD_DIM]
    m_ref[...] = jnp.full(m_ref.shape, -jnp.inf, F32)
    acc_ref[...] = jnp.zeros(acc_ref.shape, F32)

    for c in range(n_chunks):
        cols = slice(c * GQA_TK, (c + 1) * GQA_TK)
        s = jnp.dot(qs_ref[...], kt_ref[:, cols], preferred_element_type=F32)
        m_old = m_ref[...]
        m_new = jnp.maximum(m_old, jnp.max(s, axis=-1, keepdims=True))
        p = jnp.exp2(s - jnp.concatenate([m_new] * (GQA_TK // LANES), axis=1))
        alpha = jnp.exp2(m_old - m_new)
        acc_ref[...] = (jnp.concatenate([alpha, alpha], axis=1) * acc_ref[...]
                        + jnp.dot(p.astype(BF16), va_ref[cols, :], preferred_element_type=F32))
        m_ref[...] = m_new

    acc = acc_ref[...]
    o = acc[:, :HEAD_DIM] / acc[:, HEAD_DIM:HEAD_DIM + 1]
    for g in range(B_GROUP):
        sl = slice(g * HEAD_DIM, (g + 1) * HEAD_DIM)
        bg = bg_ref[:, sl].astype(F32)
        o_ref[:, sl] = (o[g * tq:(g + 1) * tq, :] * (bg * jax.nn.sigmoid(bg))).astype(o_ref.dtype)


def _gqa(h_rest, bsz, t_len):
    n_tok = h_rest.shape[0]
    nq = t_len // GQA_TQ
    rows = B_GROUP * GQA_TQ
    return pl.pallas_call(
        functools.partial(_gqa_kernel, t_len=t_len),
        grid=(bsz, B_KV_HEADS, nq),
        in_specs=[pl.BlockSpec((GQA_TQ, B_GROUP * HEAD_DIM), lambda b, h, i: (b * nq + i, GQA_Q_BLK0 + h)),
                  pl.BlockSpec((t_len, HEAD_DIM), lambda b, h, i: (b, GQA_K_BLK0 + h)),
                  pl.BlockSpec((t_len, HEAD_DIM), lambda b, h, i: (b, GQA_V_BLK0 + h)),
                  pl.BlockSpec((GQA_TQ, B_GROUP * HEAD_DIM), lambda b, h, i: (b * nq + i, GQA_BG_BLK0 + h))],
        out_specs=pl.BlockSpec((GQA_TQ, B_GROUP * HEAD_DIM), lambda b, h, i: (b * nq + i, h)),
        out_shape=jax.ShapeDtypeStruct((n_tok, ATT_WIDTH), BF16),
        scratch_shapes=[pltpu.VMEM((rows, HEAD_DIM), BF16),
                        pltpu.VMEM((HEAD_DIM, t_len), BF16),
                        pltpu.VMEM((t_len, 2 * HEAD_DIM), BF16),
                        pltpu.VMEM((rows, LANES), F32),
                        pltpu.VMEM((rows, 2 * HEAD_DIM), F32)],
        compiler_params=_cparams(("parallel", "parallel", "arbitrary")),
        name="gqa_attn",
    )(h_rest, h_rest, h_rest, h_rest)


DIL_BLOCK = 128
DIL_WIN = DIL_BLOCK + 2 * C_HALF_SPAN
DIL_TQ_MAX = 512


def _dil_kernel(*refs, tq, n_sub, has_prev, is_last):
    it = iter(refs)
    q_ref = next(it)
    km_ref, kb_ref, ka_ref = next(it), next(it), next(it)
    vm_ref, vb_ref, va_ref = next(it), next(it), next(it)
    po_ref = next(it) if has_prev else None
    pl_ref = next(it) if has_prev else None
    g_ref = next(it) if is_last else None
    o_ref = next(it)
    lse_ref = None if is_last else next(it)
    acc_scr = next(it)
    oscr = None if is_last else next(it)
    lscr = None if is_last else next(it)

    j = pl.program_id(2)
    hs = C_HALF_SPAN
    qi = lax.broadcasted_iota(jnp.int32, (DIL_BLOCK, DIL_WIN), 0)
    ci = lax.broadcasted_iota(jnp.int32, (DIL_BLOCK, DIL_WIN), 1)
    rel = ci - qi
    band = (rel >= 0) & (rel <= 2 * hs)
    lane = lax.broadcasted_iota(jnp.int32, (DIL_BLOCK, LANES), 1)
    ones = jnp.ones((DIL_WIN, HEAD_DIM), BF16)

    def window(main_ref, before_ref, after_ref, r0, sl):
        lo, hi = r0 - hs, r0 + DIL_BLOCK + hs
        parts = []
        if lo < 0:
            parts.append(before_ref[0, 0, hs + lo:hs, sl])
        parts.append(main_ref[0, 0, max(lo, 0):min(hi, tq), sl])
        if hi > tq:
            parts.append(after_ref[0, 0, 0:hi - tq, sl])
        return parts[0] if len(parts) == 1 else jnp.concatenate(parts, axis=0)

    for i in range(tq // DIL_BLOCK):
        r0 = i * DIL_BLOCK
        rows = slice(r0, r0 + DIL_BLOCK)
        pos = ci + (j * tq + r0 - hs)
        mask = band & (pos >= 0) & (pos < n_sub)
        m_tile = jnp.zeros((DIL_BLOCK, LANES), F32)
        l_tile = jnp.ones((DIL_BLOCK, LANES), F32)
        for h in range(C_HEADS):
            sl = slice(h * HEAD_DIM, (h + 1) * HEAD_DIM)
            qb = q_ref[0, 0, rows, sl]
            kw = window(km_ref, kb_ref, ka_ref, r0, sl)
            vw = jnp.concatenate([window(vm_ref, vb_ref, va_ref, r0, sl), ones], axis=1)
            s = lax.dot_general(qb, kw, (((1,), (1,)), ((), ())), preferred_element_type=F32)
            s = jnp.where(mask, s, -jnp.inf)
            m = jnp.max(s, axis=-1, keepdims=True)
            p = jnp.exp2(s - m)
            pv = jnp.dot(p.astype(BF16), vw, preferred_element_type=F32)
            acc_scr[rows, sl] = pv[:, :HEAD_DIM]
            m_tile = jnp.where(lane == h, m, m_tile)
            l_tile = jnp.where(lane == h, pv[:, HEAD_DIM:], l_tile)
        lse_t = m_tile + jnp.log2(l_tile)
        if has_prev:
            lp_t = pl_ref[0, 0, rows, :]
            mx = jnp.maximum(lp_t, lse_t)
            e1 = jnp.exp2(lp_t - mx)
            e2 = jnp.exp2(lse_t - mx)
            den = e1 + e2
            w_prev = e1 / den
            w_cur = e2 / (den * l_tile)
            lse_t = mx + jnp.log2(den)
        else:
            w_prev = None
            w_cur = 1.0 / l_tile
        for h in range(C_HEADS):
            sl = slice(h * HEAD_DIM, (h + 1) * HEAD_DIM)
            o = acc_scr[rows, sl] * w_cur[:, h:h + 1]
            if has_prev:
                o = o + po_ref[0, 0, rows, sl].astype(F32) * w_prev[:, h:h + 1]
            if is_last:
                g = g_ref[0, 0, rows, sl].astype(F32)
                o_ref[0, 0, rows, sl] = (o * (g * jax.nn.sigmoid(g))).astype(o_ref.dtype)
            else:
                oscr[h, rows, :] = o
        if not is_last:
            lscr[rows, :] = lse_t

    if not is_last:
        sub = tq // DIL_FAN
        for kap in range(DIL_FAN):
            for h in range(C_HEADS):
                o_ref[0, 0, kap, :, h * HEAD_DIM:(h + 1) * HEAD_DIM] = (
                    oscr[h, pl.ds(kap, sub, stride=DIL_FAN), :].astype(o_ref.dtype))
            lse_ref[0, 0, kap] = lscr[pl.ds(kap, sub, stride=DIL_FAN), :]


def _dil_pass(src, bsz, t_len, d, prev):
    n_sub = t_len // d
    tq = min(DIL_TQ_MAX, n_sub)
    nt = n_sub // tq
    hs = C_HALF_SPAN
    hb = tq // hs
    last_halo = n_sub // hs - 1
    width = C_HEADS * HEAD_DIM
    f = DIL_FAN
    is_last = d == C_DILATIONS[-1]
    has_prev = prev is not None

    def main(sec, w=width):
        return pl.BlockSpec((1, 1, tq, w), lambda b, r, j: (b, r, j, sec))

    def before(sec):
        return pl.BlockSpec((1, 1, hs, width), lambda b, r, j: (b, r, jnp.maximum(j * hb - 1, 0), sec))

    def after(sec):
        return pl.BlockSpec((1, 1, hs, width), lambda b, r, j: (b, r, jnp.minimum((j + 1) * hb, last_halo), sec))

    in_specs = [main(0), main(1), before(1), after(1), main(2), before(2), after(2)]
    args = [src] * 7
    if has_prev:
        in_specs += [main(0), main(0, LANES)]
        args += list(prev)
    if is_last:
        in_specs.append(main(3))
        args.append(src)
    scratch = [pltpu.VMEM((tq, width), F32)]
    if is_last:
        out_specs = main(0)
        out_shape = jax.ShapeDtypeStruct((bsz, d, n_sub, width), BF16)
    else:
        out_specs = [pl.BlockSpec((1, 1, f, tq // f, width), lambda b, r, j: (b, r, 0, j, 0)),
                     pl.BlockSpec((1, 1, f, tq // f, LANES), lambda b, r, j: (b, r, 0, j, 0))]
        out_shape = [jax.ShapeDtypeStruct((bsz, d, f, n_sub // f, width), BF16),
                     jax.ShapeDtypeStruct((bsz, d, f, n_sub // f, LANES), F32)]
        scratch += [pltpu.VMEM((C_HEADS, tq, HEAD_DIM), F32), pltpu.VMEM((tq, LANES), F32)]
    res = pl.pallas_call(
        functools.partial(_dil_kernel, tq=tq, n_sub=n_sub, has_prev=has_prev, is_last=is_last),
        grid=(bsz, d, nt),
        in_specs=in_specs,
        out_specs=out_specs,
        out_shape=out_shape,
        scratch_shapes=scratch,
        compiler_params=_cparams(("parallel", "parallel", "arbitrary")),
        name=f"dilated_d{d}",
    )(*args)
    if is_last:
        return res
    return (res[0].reshape(bsz, d * f, n_sub // f, width), res[1].reshape(bsz, d * f, n_sub // f, LANES))


def _trunk(x, params):
    bsz, t_len, _ = x.shape
    x2d = x.reshape(bsz * t_len, D_MODEL)
    p = params
    au = _proj_au(x2d, p["w_au"])
    h_rest = _proj0(x2d, p["w_rest"], p["ax_cos"][t_len], p["ax_sin"][t_len], p["qn_g"], p["kn_g"], t_len)
    y_oct = _s5_scan(au, p["s5_w_in"], p["s5_w_x"], p["s5_w_out"], p["s5_tab"], t_len)
    mix_a = _glu(y_oct, h_rest, p["glu_w"], p["glu_b"])
    mix_b = _gqa(h_rest, bsz, t_len)
    x1, x1b = _out_ln0(mix_a, mix_b, p["even_w_out"], x2d, p["ln_g0"], p["ln_b0"])
    h1, h4, h16 = _proj1(x1b, p["odd_w_in"], p["pr_cos"][t_len], p["pr_sin"][t_len], bsz, t_len)
    prev = _dil_pass(h1.reshape(bsz, 1, t_len, h1.shape[1]), bsz, t_len, 1, None)
    prev = _dil_pass(h4, bsz, t_len, DIL_FAN, prev)
    n16 = t_len // OUT_CLASSES
    mix16 = _dil_pass(h16.reshape(bsz, OUT_CLASSES, n16, h16.shape[-1]), bsz, t_len, OUT_CLASSES, prev)
    x2 = _out_ln1(mix16, p["odd_w_out"], x1, p["ln_g1"], p["ln_b1"], bsz, t_len)
    return x2.reshape(bsz, t_len, D_MODEL)


def kernel(x_prompt, x_sample, even_w_in, even_w_out, ssm_a_re, ssm_a_im, ssm_log_dt, ssm_b_re, ssm_b_im, ssm_c_re, ssm_c_im, ssm_d, ssm_glu_w, ssm_glu_b, attn_q_norm, attn_k_norm, odd_w_in, odd_w_out, ln_g, ln_b):
    c_in, c_x, c_out, tab = _s5_tables(ssm_a_re[0], ssm_a_im[0], ssm_log_dt[0], ssm_b_re[0], ssm_b_im[0],
                                       ssm_c_re[0], ssm_c_im[0], ssm_d[0])
    w_in = _s5_expand(c_in, SSM_STATE, SSM_GROUP)
    w_x = _s5_expand_lags(c_x)
    w_out = _s5_expand(c_out, SSM_GROUP, SSM_STATE)
    lens = sorted({x_prompt.shape[1], x_sample.shape[1]})
    ax = {t: _axial_tables(t) for t in lens}
    pr = {t: _partial_tables(t) for t in lens}
    params = {
        "w_au": even_w_in[0][:, :SSM_WIDTH].astype(BF16),
        "w_rest": even_w_in[0][:, SSM_WIDTH:].astype(BF16),
        "even_w_out": even_w_out[0].astype(BF16),
        "odd_w_in": odd_w_in[0].astype(BF16),
        "odd_w_out": odd_w_out[0].astype(BF16),
        "glu_w": ssm_glu_w[0].astype(BF16),
        "glu_b": ssm_glu_b[0].astype(F32).reshape(1, SSM_WIDTH),
        "qn_g": attn_q_norm[0].astype(F32).reshape(1, HEAD_DIM),
        "kn_g": attn_k_norm[0].astype(F32).reshape(1, HEAD_DIM),
        "ln_g0": ln_g[0].astype(F32).reshape(1, D_MODEL),
        "ln_b0": ln_b[0].astype(F32).reshape(1, D_MODEL),
        "ln_g1": ln_g[1].astype(F32).reshape(1, D_MODEL),
        "ln_b1": ln_b[1].astype(F32).reshape(1, D_MODEL),
        "s5_w_in": w_in, "s5_w_x": w_x, "s5_w_out": w_out, "s5_tab": tab,
        "ax_cos": {t: ax[t][0] for t in lens}, "ax_sin": {t: ax[t][1] for t in lens},
        "pr_cos": {t: pr[t][0] for t in lens}, "pr_sin": {t: pr[t][1] for t in lens},
    }
    return (_trunk(x_prompt, params), _trunk(x_sample, params))
```
